```python
import math
import jax, jax.numpy as jnp
from jax import lax
import numpy as np

D_MODEL = 1024
BATCH = 16
SEQ = 4096
DEPTH = 2
DEC_BATCH = 4
DEC_SEQ = 8192
PAST_LEN = 128

N_META = 16
GRID_W = 64
WIN_ROWS = 8
WIN_COLS = 16
NA_HEADS = 16
NA_HEAD_DIM = D_MODEL // NA_HEADS
DIFF_HEADS = 8
DIFF_HEAD_DIM = D_MODEL // (2 * DIFF_HEADS)
ROPE_THETA = 500000.0
ROPE_DIM = DIFF_HEAD_DIM // 4
D_FF = ((8 * D_MODEL // 3 + 127) // 128) * 128
Q_BLOCK = 128
RMS_EPS = 1e-6
N_MIXERS = 2
N_LAYERS_A = (DEPTH + 1) // 2
N_LAYERS_B = DEPTH // 2

kernel_name = "hybrid_natten_diffattn_macaron_encoder"


def rms_norm(x, g):
    xf = x.astype(jnp.float32)
    y = xf * lax.rsqrt(jnp.mean(xf * xf, axis=-1, keepdims=True) + RMS_EPS)
    return (y * g.astype(jnp.float32)).astype(x.dtype)


def swiglu(x, w_in, w_out):
    gate, up = jnp.split(x @ w_in, 2, axis=-1)
    return (jax.nn.silu(gate) * up) @ w_out


def neighbourhood_attention(x, w_qkv, w_o, rpb, meta_bias):
    B, L, _ = x.shape
    T = L - N_META
    rows = T // GRID_W
    wr = min(WIN_ROWS, rows)
    qkv = (x @ w_qkv).reshape(B, L, 3, NA_HEADS, NA_HEAD_DIM)
    q = qkv[:, :, 0] * (NA_HEAD_DIM ** -0.5)
    k = qkv[:, :, 1]
    v = qkv[:, :, 2]
    qm, km, vm = q[:, :N_META], k[:, :N_META], v[:, :N_META]
    kg = k[:, N_META:].reshape(B, rows, GRID_W, NA_HEADS, NA_HEAD_DIM)
    vg = v[:, N_META:].reshape(B, rows, GRID_W, NA_HEADS, NA_HEAD_DIM)
    qg = q[:, N_META:].reshape(B, rows, GRID_W, NA_HEADS, NA_HEAD_DIM).transpose(1, 0, 2, 3, 4)
    cols = jnp.arange(GRID_W)
    col_start = jnp.clip(cols - WIN_COLS // 2, 0, GRID_W - WIN_COLS)
    col_idx = col_start[:, None] + jnp.arange(WIN_COLS)[None, :]
    dc = col_idx - cols[:, None]
    mb = meta_bias.astype(jnp.float32)
    rpb_f = rpb.astype(jnp.float32)

    def row_block(args):
        r, qr = args
        r0 = jnp.clip(r - wr // 2, 0, rows - wr)
        kb = lax.dynamic_slice_in_dim(kg, r0, wr, axis=1)[:, :, col_idx]
        vb = lax.dynamic_slice_in_dim(vg, r0, wr, axis=1)[:, :, col_idx]
        dr = r0 + jnp.arange(wr) - r
        bias = rpb_f[:, dr[:, None, None] + WIN_ROWS - 1, dc[None] + WIN_COLS - 1]
        bias = bias.transpose(0, 2, 1, 3)
        s_loc = jnp.einsum('bqhd,bwqjhd->bhqwj', qr, kb).astype(jnp.float32) + bias[None]
        s_meta = jnp.einsum('bqhd,bmhd->bhqm', qr, km).astype(jnp.float32) + mb[None, :, None, :]
        s = jnp.concatenate([s_loc.reshape(B, NA_HEADS, GRID_W, wr * WIN_COLS), s_meta], axis=-1)
        p = jax.nn.softmax(s, axis=-1).astype(v.dtype)
        p_loc = p[..., :wr * WIN_COLS].reshape(B, NA_HEADS, GRID_W, wr, WIN_COLS)
        p_meta = p[..., wr * WIN_COLS:]
        return (jnp.einsum('bhqwj,bwqjhd->bqhd', p_loc, vb)
                + jnp.einsum('bhqm,bmhd->bqhd', p_meta, vm))

    og = lax.map(row_block, (jnp.arange(rows), qg))
    o_real = og.transpose(1, 0, 2, 3, 4).reshape(B, T, D_MODEL)
    s_mm = jnp.einsum('bqhd,bmhd->bhqm', qm, km).astype(jnp.float32) + mb[None, :, None, :]
    p_mm = jax.nn.softmax(s_mm, axis=-1).astype(v.dtype)
    o_meta = jnp.einsum('bhqm,bmhd->bqhd', p_mm, vm).reshape(B, N_META, D_MODEL)
    o = jnp.concatenate([o_meta, o_real], axis=1)
    return o @ w_o


def rope_cos_sin(L):
    inv = ROPE_THETA ** (-jnp.arange(0, ROPE_DIM, 2, dtype=jnp.float32) / ROPE_DIM)
    ang = jnp.arange(L, dtype=jnp.float32)[:, None] * inv[None, :]
    return jnp.cos(ang), jnp.sin(ang)


def apply_partial_rope(x, cos, sin):
    half = ROPE_DIM // 2
    x1 = x[..., :half].astype(jnp.float32)
    x2 = x[..., half:ROPE_DIM].astype(jnp.float32)
    c = cos[None, :, None, None, :]
    s = sin[None, :, None, None, :]
    rot = jnp.concatenate([x1 * c - x2 * s, x2 * c + x1 * s], axis=-1).astype(x.dtype)
    return jnp.concatenate([rot, x[..., ROPE_DIM:]], axis=-1)


def diff_attention(x, w_qkv, w_o, lam, subln, lambda_init):
    B, L, _ = x.shape
    qkv = x @ w_qkv
    q = qkv[..., :D_MODEL].reshape(B, L, DIFF_HEADS, 2, DIFF_HEAD_DIM)
    k = qkv[..., D_MODEL:2 * D_MODEL].reshape(B, L, DIFF_HEADS, 2, DIFF_HEAD_DIM)
    v = qkv[..., 2 * D_MODEL:].reshape(B, L, DIFF_HEADS, 2 * DIFF_HEAD_DIM)
    cos, sin = rope_cos_sin(L)
    q = apply_partial_rope(q, cos, sin) * (DIFF_HEAD_DIM ** -0.5)
    k = apply_partial_rope(k, cos, sin)
    lf = lam.astype(jnp.float32)
    lam_full = jnp.exp(jnp.sum(lf[0] * lf[1])) - jnp.exp(jnp.sum(lf[2] * lf[3])) + lambda_init

    def attend(qb):
        s = jnp.einsum('bqhmd,bkhmd->bhmqk', qb, k).astype(jnp.float32)
        p = jax.nn.softmax(s, axis=-1)
        a = (p[:, :, 0] - lam_full * p[:, :, 1]).astype(v.dtype)
        return jnp.einsum('bhqk,bkhe->bqhe', a, v)

    T = L - N_META
    o_meta = attend(q[:, :N_META])
    qr = q[:, N_META:].reshape(B, T // Q_BLOCK, Q_BLOCK, DIFF_HEADS, 2, DIFF_HEAD_DIM)
    o_real = lax.map(attend, qr.transpose(1, 0, 2, 3, 4, 5))
    o_real = o_real.transpose(1, 0, 2, 3, 4).reshape(B, T, DIFF_HEADS, 2 * DIFF_HEAD_DIM)
    o = jnp.concatenate([o_meta, o_real], axis=1)
    o = rms_norm(o, subln) * (1.0 - lambda_init)
    return o.reshape(B, L, D_MODEL) @ w_o


def run_trunk(x, meta_tokens, norm_g, w_ffn_in, w_ffn_out, w_qkv_a, w_o_a, rpb_a,
              meta_bias_a, w_qkv_b, w_o_b, lambda_b, subln_b):
    B = x.shape[0]
    meta = jnp.broadcast_to(meta_tokens.astype(x.dtype)[None], (B, N_META, D_MODEL))
    h = jnp.concatenate([meta, x], axis=1)
    for i in range(DEPTH):
        g = norm_g[i]
        h = h + 0.5 * rms_norm(swiglu(rms_norm(h, g[0]), w_ffn_in[i, 0], w_ffn_out[i, 0]), g[1])
        u = rms_norm(h, g[2])
        j = i // N_MIXERS
        if i % N_MIXERS == 0:
            m = neighbourhood_attention(u, w_qkv_a[j], w_o_a[j], rpb_a[j], meta_bias_a[j])
        else:
            lambda_init = 0.8 - 0.6 * math.exp(-0.3 * i)
            m = diff_attention(u, w_qkv_b[j], w_o_b[j], lambda_b[j], subln_b[j], lambda_init)
        h = h + rms_norm(m, g[3])
        h = h + 0.5 * rms_norm(swiglu(rms_norm(h, g[4]), w_ffn_in[i, 1], w_ffn_out[i, 1]), g[5])
    return h[:, N_META:]


def setup_inputs(seed: int = 0) -> dict:
    key = jax.random.key(seed)
    ks = jax.random.split(key, 16)
    f32 = jnp.float32
    nrm = lambda k, shape, s: jax.random.normal(k, shape, f32) * s
    return {
        "x_prompt": nrm(ks[0], (BATCH, SEQ, D_MODEL), 1.0),
        "x_sample": nrm(ks[1], (DEC_BATCH, DEC_SEQ, D_MODEL), 1.0),
        "meta_tokens": nrm(ks[2], (N_META, D_MODEL), 1.0),
        "norm_g": 1.0 + nrm(ks[3], (DEPTH, 6, D_MODEL), 0.1),
        "w_ffn_in": nrm(ks[4], (DEPTH, 2, D_MODEL, 2 * D_FF), D_MODEL ** -0.5),
        "w_ffn_out": nrm(ks[5], (DEPTH, 2, D_FF, D_MODEL), D_FF ** -0.5),
        "w_qkv_a": nrm(ks[6], (N_LAYERS_A, D_MODEL, 3 * D_MODEL), D_MODEL ** -0.5),
        "w_o_a": nrm(ks[7], (N_LAYERS_A, D_MODEL, D_MODEL), D_MODEL ** -0.5),
        "rpb_a": nrm(ks[8], (N_LAYERS_A, NA_HEADS, 2 * WIN_ROWS - 1, 2 * WIN_COLS - 1), 0.1),
        "meta_bias_a": nrm(ks[9], (N_LAYERS_A, NA_HEADS, N_META), 0.1),
        "w_qkv_b": nrm(ks[10], (N_LAYERS_B, D_MODEL, 3 * D_MODEL), D_MODEL ** -0.5),
        "w_o_b": nrm(ks[11], (N_LAYERS_B, D_MODEL, D_MODEL), D_MODEL ** -0.5),
        "lambda_b": nrm(ks[12], (N_LAYERS_B, 4, DIFF_HEAD_DIM), 0.1),
        "subln_b": 1.0 + nrm(ks[13], (N_LAYERS_B, 2 * DIFF_HEAD_DIM), 0.1),
    }


def reference(x_prompt, x_sample, meta_tokens, norm_g, w_ffn_in, w_ffn_out, w_qkv_a, w_o_a,
              rpb_a, meta_bias_a, w_qkv_b, w_o_b, lambda_b, subln_b):
    y_prompt = run_trunk(x_prompt, meta_tokens, norm_g, w_ffn_in, w_ffn_out, w_qkv_a, w_o_a,
                         rpb_a, meta_bias_a, w_qkv_b, w_o_b, lambda_b, subln_b)
    y_sample = run_trunk(x_sample, meta_tokens, norm_g, w_ffn_in, w_ffn_out, w_qkv_a, w_o_a,
                         rpb_a, meta_bias_a, w_qkv_b, w_o_b, lambda_b, subln_b)
    return (y_prompt, y_sample)
```

```python
import functools
import math

import jax
import jax.numpy as jnp
from jax import lax
from jax.experimental import pallas as pl
from jax.experimental.pallas import tpu as pltpu

F32 = jnp.float32
BF16 = jnp.bfloat16

D_MODEL = 1024
N_META = 16
GRID_W = 64
WIN_ROWS = 8
WIN_COLS = 16
NA_HEADS = 16
NA_HEAD_DIM = D_MODEL // NA_HEADS
DIFF_HEADS = 8
DIFF_HEAD_DIM = D_MODEL // (2 * DIFF_HEADS)
ROPE_THETA = 500000.0
ROPE_DIM = DIFF_HEAD_DIM // 4
RMS_EPS = 1e-6
N_MIXERS = 2

LANES = 128
N_PAIRS = D_MODEL // LANES
VMEM_LIMIT = 56 * 1024 * 1024
MASKED = -1e30

DENSE_TM = 512

NA_RB = 4
NA_KR = NA_RB + WIN_ROWS
NA_TQ = NA_RB * GRID_W
NA_TKL = NA_KR * GRID_W
NA_TK = NA_TKL + LANES

DIFF_TQ = 512
DIFF_TK = DENSE_TM
DIFF_TQ_META = 128


def _rms(x, g):
    ms = jnp.mean(x * x, axis=-1, keepdims=True)
    return x * lax.rsqrt(ms + RMS_EPS) * g


def _dot(a, b):
    return jnp.dot(a, b, preferred_element_type=F32)


def _dot_nt(a, b):
    return lax.dot_general(a, b, (((1,), (1,)), ((), ())), preferred_element_type=F32)


def _resident(shape):
    nd = len(shape)
    return pl.BlockSpec(shape, lambda *_: (0,) * nd, pipeline_mode=pl.Buffered(1))


def _params(*sem):
    return pltpu.CompilerParams(dimension_semantics=sem, vmem_limit_bytes=VMEM_LIMIT)


def _ffn_kernel(h_ref, ga_ref, gb_ref, win_ref, wout_ref, o_ref):
    x = h_ref[...]
    xn = _rms(x, ga_ref[...]).astype(BF16)
    gu = _dot(xn, win_ref[...])
    d_ff = wout_ref.shape[0]
    gate = gu[:, :d_ff]
    act = (gate * jax.nn.sigmoid(gate) * gu[:, d_ff:]).astype(BF16)
    y = _dot(act, wout_ref[...])
    o_ref[...] = x + 0.5 * _rms(y, gb_ref[...])


def _ffn(h, ga, gb, w_in, w_out, tm):
    n = h.shape[0]
    tile = pl.BlockSpec((tm, D_MODEL), lambda i: (i, 0))
    return pl.pallas_call(
        _ffn_kernel,
        out_shape=jax.ShapeDtypeStruct((n, D_MODEL), F32),
        grid=(n // tm,),
        in_specs=[tile, _resident((1, D_MODEL)), _resident((1, D_MODEL)),
                  _resident(w_in.shape), _resident(w_out.shape)],
        out_specs=tile,
        compiler_params=_params("parallel"),
        name="ffn",
    )(h, ga, gb, w_in, w_out)


def _out_kernel(h_ref, a_ref, w_ref, g_ref, o_ref):
    o_ref[...] = h_ref[...] + _rms(_dot(a_ref[...], w_ref[...]), g_ref[...])


def _out_proj(h, a, w_o, g, tm):
    n = h.shape[0]
    tile = pl.BlockSpec((tm, D_MODEL), lambda i: (i, 0))
    return pl.pallas_call(
        _out_kernel,
        out_shape=jax.ShapeDtypeStruct((n, D_MODEL), F32),
        grid=(n // tm,),
        in_specs=[tile, tile, _resident(w_o.shape), _resident((1, D_MODEL))],
        out_specs=tile,
        compiler_params=_params("parallel"),
        name="out_proj",
    )(h, a, w_o, g)


def _qkv_kernel(*refs, rope):
    if rope:
        h_ref, g_ref, wqk_ref, wvt_ref, c_ref, s1_ref, s2_ref, q_ref, k_ref, vt_ref = refs
    else:
        h_ref, g_ref, wqk_ref, wvt_ref, q_ref, k_ref, vt_ref = refs
    xn = _rms(h_ref[...], g_ref[...]).astype(BF16)
    qk = _dot(xn, wqk_ref[...])
    vt = _dot_nt(wvt_ref[...], xn)
    scale = NA_HEAD_DIM ** -0.5
    if not rope:
        q_ref[...] = (qk[:, :D_MODEL] * scale).astype(BF16)
        k_ref[...] = qk[:, D_MODEL:].astype(BF16)
        vt_ref[...] = vt.astype(BF16)
        return
    c, s1, s2 = c_ref[...], s1_ref[...], s2_ref[...]

    def rot(x):
        return x * c + pltpu.roll(x, LANES - ROPE_DIM // 2, 1) * s1 + pltpu.roll(x, ROPE_DIM // 2, 1) * s2

    for hd in range(N_PAIRS):
        lo = hd * LANES
        q_ref[hd] = (rot(qk[:, lo:lo + LANES]) * scale).astype(BF16)
        k_ref[hd] = rot(qk[:, D_MODEL + lo:D_MODEL + lo + LANES]).astype(BF16)
        vt_ref[hd, 0] = vt[lo:lo + LANES, :].astype(BF16)


def _qkv(h, g, w_qk, w_vt, tm, rope_tables=None, rope_period=None):
    n = h.shape[0]
    nt = n // tm
    tile = pl.BlockSpec((tm, D_MODEL), lambda i: (i, 0))
    in_specs = [tile, _resident((1, D_MODEL)), _resident(w_qk.shape), _resident(w_vt.shape)]
    args = [h, g, w_qk, w_vt]
    if rope_tables is None:
        out_shape = (jax.ShapeDtypeStruct((n, D_MODEL), BF16), jax.ShapeDtypeStruct((n, D_MODEL), BF16),
                     jax.ShapeDtypeStruct((D_MODEL, n), BF16))
        out_specs = (tile, tile, pl.BlockSpec((D_MODEL, tm), lambda i: (0, i)))
    else:
        tab = pl.BlockSpec((tm, LANES), lambda i: (i % rope_period, 0))
        in_specs += [tab, tab, tab]
        args += list(rope_tables)
        hm = pl.BlockSpec((N_PAIRS, tm, LANES), lambda i: (0, i, 0))
        out_shape = (jax.ShapeDtypeStruct((N_PAIRS, n, LANES), BF16), jax.ShapeDtypeStruct((N_PAIRS, n, LANES), BF16),
                     jax.ShapeDtypeStruct((N_PAIRS, nt, LANES, tm), BF16))
        out_specs = (hm, hm, pl.BlockSpec((N_PAIRS, 1, LANES, tm), lambda i: (0, i, 0, 0)))
    return pl.pallas_call(
        functools.partial(_qkv_kernel, rope=rope_tables is not None),
        out_shape=out_shape,
        grid=(nt,),
        in_specs=in_specs,
        out_specs=out_specs,
        compiler_params=_params("parallel"),
        name="qkv_rope" if rope_tables is not None else "qkv",
    )(*args)


def _na_kernel(q_ref, k0_ref, k1_ref, k2_ref, v0_ref, v1_ref, v2_ref, qm_ref, km_ref, vtm_ref,
               bias_ref, mb_ref, o_ref, om_ref, kw_ref, vtw_ref):
    rb = pl.program_id(1)
    kw_ref[0:NA_TQ] = k0_ref[...]
    kw_ref[NA_TQ:2 * NA_TQ] = k1_ref[...]
    kw_ref[2 * NA_TQ:NA_TKL] = k2_ref[...]
    kw_ref[NA_TKL:NA_TK] = jnp.zeros((LANES, D_MODEL), BF16)
    kw_ref[NA_TKL:NA_TKL + N_META] = km_ref[...]
    vtw_ref[:, 0:NA_TQ] = v0_ref[...]
    vtw_ref[:, NA_TQ:2 * NA_TQ] = v1_ref[...]
    vtw_ref[:, 2 * NA_TQ:NA_TKL] = v2_ref[...]
    vtw_ref[:, NA_TKL:NA_TK] = jnp.zeros((D_MODEL, LANES), BF16)
    vtw_ref[:, NA_TKL:NA_TKL + N_META] = vtm_ref[0]

    lane = lax.broadcasted_iota(jnp.int32, (1, LANES), 1)
    row = lax.broadcasted_iota(jnp.int32, (LANES, 1), 0)
    even = lane < NA_HEAD_DIM
    zero = jnp.zeros((), BF16)

    for p in range(N_PAIRS):
        lo = p * LANES
        qp = q_ref[:, lo:lo + LANES]
        qa = jnp.concatenate([jnp.where(even, qp, zero), jnp.where(even, zero, qp)], axis=0)
        s = _dot_nt(kw_ref[:, lo:lo + LANES], qa) + bias_ref[0, p]
        e = jnp.exp(s - jnp.max(s, axis=0, keepdims=True))
        den = jnp.sum(e, axis=0, keepdims=True)
        ot = _dot(vtw_ref[lo:lo + LANES, :], e.astype(BF16)) / den
        o = jnp.where(row < NA_HEAD_DIM, ot[:, :NA_TQ], ot[:, NA_TQ:])
        o_ref[:, lo:lo + LANES] = o.T.astype(BF16)

    @pl.when(rb == 0)
    def _meta_queries():
        for p in range(N_PAIRS):
            lo = p * LANES
            qm = qm_ref[:, lo:lo + LANES]
            kmp = km_ref[:, lo:lo + LANES]
            vtp = vtm_ref[0, lo:lo + LANES, :]
            outs = []
            for half in range(2):
                qx = jnp.where(even, qm, zero) if half == 0 else jnp.where(even, zero, qm)
                hd = 2 * p + half
                s = _dot_nt(qx, kmp) + mb_ref[hd:hd + 1, :]
                e = jnp.exp(s - jnp.max(s, axis=-1, keepdims=True))
                pr = e / jnp.sum(e, axis=-1, keepdims=True)
                outs.append(_dot_nt(pr.astype(BF16), vtp))
            om_ref[:, lo:lo + LANES] = jnp.where(even, outs[0], outs[1]).astype(BF16)


def _na_bias(rpb, meta_bias):
    i = jnp.arange(NA_RB)[:, None]
    j = jnp.arange(NA_KR)[None, :]
    dr = jnp.stack([j - i, j - i - WIN_ROWS // 2, j - i - WIN_ROWS])
    row_ok = jnp.stack([jnp.broadcast_to(j < WIN_ROWS, (NA_RB, NA_KR)),
                        (j - i >= 0) & (j - i < WIN_ROWS),
                        jnp.broadcast_to(j >= NA_KR - WIN_ROWS, (NA_RB, NA_KR))])
    qc = jnp.arange(GRID_W)[:, None]
    kc = jnp.arange(GRID_W)[None, :]
    c0 = jnp.clip(qc - WIN_COLS // 2, 0, GRID_W - WIN_COLS)
    col_ok = (kc >= c0) & (kc < c0 + WIN_COLS)
    dc = kc - qc
    ri = jnp.clip(dr, 1 - WIN_ROWS, WIN_ROWS - 1) + WIN_ROWS - 1
    ci = jnp.clip(dc, 1 - WIN_COLS, WIN_COLS - 1) + WIN_COLS - 1
    vals = rpb.astype(F32)[:, ri[:, :, None, :, None], ci[None, None, :, None, :]]
    ok = row_ok[:, :, None, :, None] & col_ok[None, None, :, None, :]
    loc = jnp.where(ok[None], vals, MASKED).reshape(NA_HEADS, 3, NA_TQ, NA_TKL)
    meta = jnp.broadcast_to(meta_bias.astype(F32)[:, None, None, :], (NA_HEADS, 3, NA_TQ, N_META))
    pad = jnp.full((NA_HEADS, 3, NA_TQ, LANES - N_META), MASKED, F32)
    full = jnp.concatenate([loc, meta, pad], axis=-1)
    full = full.reshape(N_PAIRS, 2, 3, NA_TQ, NA_TK)
    return full.transpose(2, 0, 4, 1, 3).reshape(3, N_PAIRS, NA_TK, 2 * NA_TQ)


def _na_attention(q, k, vt, qm, km, vtm, bias, mb, batch, seq, boff):
    n = seq // NA_TQ
    assert seq % NA_TQ == 0 and n >= 3

    def w0(rb):
        return jnp.clip(rb - 1, 0, n - 3)

    def kspec(jj):
        return pl.BlockSpec((NA_TQ, D_MODEL), lambda b, rb: (b * n + w0(rb) + jj, 0))

    def vspec(jj):
        return pl.BlockSpec((D_MODEL, NA_TQ), lambda b, rb: (0, b * n + w0(rb) + jj))

    mrow = pl.BlockSpec((N_META, D_MODEL), lambda b, rb: (boff + b, 0))
    cls = lambda b, rb: (jnp.where(rb == 0, 0, jnp.where(rb == n - 1, 2, 1)), 0, 0, 0)
    return pl.pallas_call(
        _na_kernel,
        out_shape=(jax.ShapeDtypeStruct((batch * seq, D_MODEL), BF16),
                   jax.ShapeDtypeStruct((batch * N_META, D_MODEL), BF16)),
        grid=(batch, n),
        in_specs=[pl.BlockSpec((NA_TQ, D_MODEL), lambda b, rb: (b * n + rb, 0)),
                  kspec(0), kspec(1), kspec(2), vspec(0), vspec(1), vspec(2),
                  mrow, mrow, pl.BlockSpec((1, D_MODEL, N_META), lambda b, rb: (boff + b, 0, 0)),
                  pl.BlockSpec((1, N_PAIRS, NA_TK, 2 * NA_TQ), cls, pipeline_mode=pl.Buffered(1)),
                  _resident(mb.shape)],
        out_specs=(pl.BlockSpec((NA_TQ, D_MODEL), lambda b, rb: (b * n + rb, 0)),
                   pl.BlockSpec((N_META, D_MODEL), lambda b, rb: (b, 0))),
        scratch_shapes=[pltpu.VMEM((NA_TK, D_MODEL), BF16), pltpu.VMEM((D_MODEL, NA_TK), BF16)],
        compiler_params=_params("parallel", "arbitrary"),
        name="na_attention",
    )(q, k, k, k, vt, vt, vt, qm, km, vtm, bias, mb)


def _diff_kernel(q_ref, k_ref, vt_ref, km_ref, vtm_ref, lam_ref, subln_ref, o_ref, acc_ref, *, lambda_init):
    tq = q_ref.shape[1]
    nkv = vt_ref.shape[1]
    tk = vt_ref.shape[3]
    lane = lax.broadcasted_iota(jnp.int32, (1, LANES), 1)
    first = lane < DIFF_HEAD_DIM
    zero = jnp.zeros((), BF16)
    q = q_ref[0]
    qa = jnp.concatenate([jnp.where(first, q, zero), jnp.where(first, zero, q)], axis=0)

    s = _dot_nt(km_ref[0], qa)
    m0 = jnp.max(s, axis=0, keepdims=True)
    e = jnp.exp(s - m0)
    l0 = jnp.sum(e, axis=0, keepdims=True)
    acc_ref[...] = _dot(vtm_ref[0, 0], e.astype(BF16))

    def body(kv, carry):
        m, l = carry
        start = pl.multiple_of(kv * tk, tk)
        s = _dot_nt(k_ref[0, pl.ds(start, tk), :], qa)
        m_new = jnp.maximum(m, jnp.max(s, axis=0, keepdims=True))
        alpha = jnp.exp(m - m_new)
        e = jnp.exp(s - m_new)
        acc_ref[...] = alpha * acc_ref[...] + _dot(vt_ref[0, kv], e.astype(BF16))
        return m_new, alpha * l + jnp.sum(e, axis=0, keepdims=True)

    _, l = lax.fori_loop(0, nkv, body, (m0, l0))

    lf = lam_ref[...]
    lam = (jnp.exp(jnp.sum(lf[0:1] * lf[1:2], axis=-1, keepdims=True))
           - jnp.exp(jnp.sum(lf[2:3] * lf[3:4], axis=-1, keepdims=True)) + lambda_init)
    on = acc_ref[...] / l
    ot = on[:, :tq] - lam * on[:, tq:]
    ms = jnp.mean(ot * ot, axis=0, keepdims=True)
    y = ot * lax.rsqrt(ms + RMS_EPS) * subln_ref[...] * (1.0 - lambda_init)
    o_ref[...] = y.T.astype(BF16)


def _diff_attention(q, k, vt, km, vtm, lam, subln, batch, seq, boff, tq, nq, q_boff, lambda_init):
    nkv = seq // DIFF_TK
    return pl.pallas_call(
        functools.partial(_diff_kernel, lambda_init=lambda_init),
        out_shape=jax.ShapeDtypeStruct((batch * nq * tq, D_MODEL), BF16),
        grid=(batch, DIFF_HEADS, nq),
        in_specs=[pl.BlockSpec((1, tq, LANES), lambda b, h, qi: (h, (q_boff + b) * nq + qi, 0)),
                  pl.BlockSpec((1, seq, LANES), lambda b, h, qi: (h, b, 0)),
                  pl.BlockSpec((1, nkv, LANES, DIFF_TK), lambda b, h, qi: (h, b, 0, 0)),
                  pl.BlockSpec((1, N_META, LANES), lambda b, h, qi: (h, boff + b, 0)),
                  pl.BlockSpec((1, 1, LANES, N_META), lambda b, h, qi: (boff + b, h, 0, 0)),
                  _resident(lam.shape), _resident(subln.shape)],
        out_specs=pl.BlockSpec((tq, LANES), lambda b, h, qi: (b * nq + qi, h)),
        scratch_shapes=[pltpu.VMEM((LANES, 2 * tq), F32)],
        compiler_params=_params("parallel", "parallel", "arbitrary"),
        name="diff_attention",
    )(q, k, vt, km, vtm, lam, subln)


def _rope_tables(pos):
    half = ROPE_DIM // 2
    inv = ROPE_THETA ** (-jnp.arange(0, ROPE_DIM, 2, dtype=F32) / ROPE_DIM)
    ang = pos.astype(F32)[:, None] * inv[None, :]
    cos, sin = jnp.cos(ang), jnp.sin(ang)
    n = pos.shape[0]
    rest = DIFF_HEAD_DIM - ROPE_DIM
    z, zr, one = jnp.zeros((n, half), F32), jnp.zeros((n, rest), F32), jnp.ones((n, rest), F32)
    c = jnp.concatenate([cos, cos, one], axis=1)
    s1 = jnp.concatenate([-sin, z, zr], axis=1)
    s2 = jnp.concatenate([z, sin, zr], axis=1)
    return tuple(jnp.concatenate([t, t], axis=1) for t in (c, s1, s2))


def kernel(x_prompt, x_sample, meta_tokens, norm_g, w_ffn_in, w_ffn_out, w_qkv_a, w_o_a, rpb_a,
           meta_bias_a, w_qkv_b, w_o_b, lambda_b, subln_b):
    xs = (x_prompt, x_sample)
    batches = tuple(x.shape[0] for x in xs)
    seqs = tuple(x.shape[1] for x in xs)
    boffs = (0, batches[0])
    nb = sum(batches)
    depth = norm_g.shape[0]
    for t in seqs:
        assert t % DENSE_TM == 0 and t % DIFF_TQ == 0 and t % GRID_W == 0

    hs = [x.reshape(-1, D_MODEL).astype(F32) for x in xs]
    hs.append(jnp.broadcast_to(meta_tokens.astype(F32)[None], (nb, N_META, D_MODEL)).reshape(nb * N_META, D_MODEL))
    tms = (DENSE_TM, DENSE_TM, nb * N_META)

    w_in = w_ffn_in.astype(BF16)
    w_out = w_ffn_out.astype(BF16)
    g = norm_g.astype(F32)[:, :, None, :]

    max_seq = max(seqs)
    real_tabs = _rope_tables(N_META + jnp.arange(max_seq))
    meta_tabs = _rope_tables(jnp.tile(jnp.arange(N_META), nb))

    for i in range(depth):
        j = i // N_MIXERS
        hs = [_ffn(h, g[i, 0], g[i, 1], w_in[i, 0], w_out[i, 0], tm) for h, tm in zip(hs, tms)]
        if i % N_MIXERS == 0:
            w = w_qkv_a[j]
            w_qk = w[:, :2 * D_MODEL].astype(BF16)
            w_vt = w[:, 2 * D_MODEL:].T.astype(BF16)
            w_o = w_o_a[j].astype(BF16)
            qkv = [_qkv(h, g[i, 2], w_qk, w_vt, tm) for h, tm in zip(hs, tms)]
            qm, km, vtm = qkv[2]
            vtm = vtm.reshape(D_MODEL, nb, N_META).transpose(1, 0, 2)
            bias = _na_bias(rpb_a[j], meta_bias_a[j])
            mb = meta_bias_a[j].astype(F32)
            outs = [_na_attention(*qkv[gi], qm, km, vtm, bias, mb, batches[gi], seqs[gi], boffs[gi])
                    for gi in range(2)]
            attn = [outs[0][0], outs[1][0], jnp.concatenate([outs[0][1], outs[1][1]], axis=0)]
        else:
            lambda_init = 0.8 - 0.6 * math.exp(-0.3 * i)
            w = w_qkv_b[j]
            w_qk = w[:, :2 * D_MODEL].astype(BF16)
            w_vt = w[:, 2 * D_MODEL:].T.astype(BF16)
            w_o = w_o_b[j].astype(BF16)
            qkv = [_qkv(hs[gi], g[i, 2], w_qk, w_vt, DENSE_TM, real_tabs, seqs[gi] // DENSE_TM) for gi in range(2)]
            qm, km, vtm = _qkv(hs[2], g[i, 2], w_qk, w_vt, tms[2], meta_tabs, 1)
            vtm = vtm.reshape(N_PAIRS, LANES, nb, N_META).transpose(2, 0, 1, 3)
            qm_pad = jnp.pad(qm.reshape(N_PAIRS, nb, N_META, LANES),
                             ((0, 0), (0, 0), (0, DIFF_TQ_META - N_META), (0, 0)))
            qm_pad = qm_pad.reshape(N_PAIRS, nb * DIFF_TQ_META, LANES)
            lam = lambda_b[j].astype(F32)
            subln = subln_b[j].astype(F32).reshape(LANES, 1)
            attn, attn_meta = [], []
            for gi in range(2):
                q, k, vt = qkv[gi]
                common = (k, vt, km, vtm, lam, subln, batches[gi], seqs[gi], boffs[gi])
                attn.append(_diff_attention(q, *common, DIFF_TQ, seqs[gi] // DIFF_TQ, 0, lambda_init))
                om = _diff_attention(qm_pad, *common, DIFF_TQ_META, 1, boffs[gi], lambda_init)
                attn_meta.append(om.reshape(batches[gi], DIFF_TQ_META, D_MODEL)[:, :N_META].reshape(-1, D_MODEL))
            attn.append(jnp.concatenate(attn_meta, axis=0))
        hs = [_out_proj(h, a, w_o, g[i, 3], tm) for h, a, tm in zip(hs, attn, tms)]
        hs = [_ffn(h, g[i, 4], g[i, 5], w_in[i, 1], w_out[i, 1], tm) for h, tm in zip(hs, tms)]

    return tuple(h.reshape(x.shape).astype(x.dtype) for h, x in zip(hs[:2], xs))
```

```python
import functools
import math

import jax
import jax.numpy as jnp
from jax import lax
from jax.experimental import pallas as pl
from jax.experimental.pallas import tpu as pltpu

F32 = jnp.float32
BF16 = jnp.bfloat16

D_MODEL = 1024
N_META = 16
GRID_W = 64
WIN_ROWS = 8
WIN_COLS = 16
NA_HEADS = 16
NA_HEAD_DIM = D_MODEL // NA_HEADS
DIFF_HEADS = 8
DIFF_HEAD_DIM = D_MODEL // (2 * DIFF_HEADS)
ROPE_THETA = 500000.0
ROPE_DIM = DIFF_HEAD_DIM // 4
RMS_EPS = 1e-6
N_MIXERS = 2

LANES = 128
N_PAIRS = D_MODEL // LANES
VMEM_LIMIT = 56 * 1024 * 1024
MASKED = -1e30
LOG2E = 1.4426950408889634
SUM_ROWS = 16
VT_ROWS = LANES + SUM_ROWS

DENSE_TM = 512

NA_RB = 4
NA_KR = NA_RB + WIN_ROWS
NA_TQ = NA_RB * GRID_W
NA_TKL = NA_KR * GRID_W
NA_TK = NA_TKL + LANES

DIFF_TQ = 512
DIFF_TK = DENSE_TM
DIFF_TQ_META = 128
DIFF_UNROLL = 4


def _rms(x, g):
    ms = jnp.mean(x * x, axis=-1, keepdims=True)
    return x * lax.rsqrt(ms + RMS_EPS) * g


def _dot(a, b):
    return jnp.dot(a, b, preferred_element_type=F32)


def _dot_nt(a, b):
    return lax.dot_general(a, b, (((1,), (1,)), ((), ())), preferred_element_type=F32)


def _resident(shape):
    nd = len(shape)
    return pl.BlockSpec(shape, lambda *_: (0,) * nd, pipeline_mode=pl.Buffered(1))


def _params(*sem):
    return pltpu.CompilerParams(dimension_semantics=sem, vmem_limit_bytes=VMEM_LIMIT)


def _split_maps(x):
    first = lax.broadcasted_iota(jnp.int32, (1, LANES), 1) < LANES // 2
    zero = jnp.zeros((), x.dtype)
    return jnp.concatenate([jnp.where(first, x, zero), jnp.where(first, zero, x)], axis=0)


def _half_ffn(x, ga, gb, win_ref, wout_ref):
    xn = _rms(x, ga).astype(BF16)
    gu = _dot(xn, win_ref[...])
    d_ff = wout_ref.shape[0]
    gate = gu[:, :d_ff]
    act = (gate * jax.nn.sigmoid(gate) * gu[:, d_ff:]).astype(BF16)
    y = _dot(act, wout_ref[...])
    return x + 0.5 * _rms(y, gb)


def _ffn_kernel(h_ref, ga_ref, gb_ref, win_ref, wout_ref, o_ref):
    o_ref[...] = _half_ffn(h_ref[...], ga_ref[...], gb_ref[...], win_ref, wout_ref)


def _ffn(h, ga, gb, w_in, w_out, tm):
    n = h.shape[0]
    tile = pl.BlockSpec((tm, D_MODEL), lambda i: (i, 0))
    return pl.pallas_call(
        _ffn_kernel,
        out_shape=jax.ShapeDtypeStruct((n, D_MODEL), F32),
        grid=(n // tm,),
        in_specs=[tile, _resident((1, D_MODEL)), _resident((1, D_MODEL)),
                  _resident(w_in.shape), _resident(w_out.shape)],
        out_specs=tile,
        compiler_params=_params("parallel"),
        name="ffn",
    )(h, ga, gb, w_in, w_out)


def _mix_ffn_kernel(h_ref, a_ref, wo_ref, gm_ref, ga_ref, gb_ref, win_ref, wout_ref, o_ref):
    x = h_ref[...] + _rms(_dot(a_ref[...], wo_ref[...]), gm_ref[...])
    o_ref[...] = _half_ffn(x, ga_ref[...], gb_ref[...], win_ref, wout_ref)


def _mix_ffn(h, a, w_o, gm, ga, gb, w_in, w_out, tm):
    n = h.shape[0]
    tile = pl.BlockSpec((tm, D_MODEL), lambda i: (i, 0))
    vec = _resident((1, D_MODEL))
    return pl.pallas_call(
        _mix_ffn_kernel,
        out_shape=jax.ShapeDtypeStruct((n, D_MODEL), F32),
        grid=(n // tm,),
        in_specs=[tile, tile, _resident(w_o.shape), vec, vec, vec, _resident(w_in.shape), _resident(w_out.shape)],
        out_specs=tile,
        compiler_params=_params("parallel"),
        name="mix_ffn",
    )(h, a, w_o, gm, ga, gb, w_in, w_out)


def _qkv_kernel(*refs, rope):
    if rope:
        h_ref, g_ref, wqk_ref, wvt_ref, c_ref, s1_ref, s2_ref, q_ref, k_ref, vt_ref = refs
    else:
        h_ref, g_ref, wqk_ref, wvt_ref, q_ref, k_ref, vt_ref = refs
    xn = _rms(h_ref[...], g_ref[...]).astype(BF16)
    qk = _dot(xn, wqk_ref[...])
    vt = _dot_nt(wvt_ref[...], xn)
    scale = NA_HEAD_DIM ** -0.5 * LOG2E
    if not rope:
        q_ref[...] = (qk[:, :D_MODEL] * scale).astype(BF16)
        k_ref[...] = qk[:, D_MODEL:].astype(BF16)
        vt_ref[...] = vt.astype(BF16)
        return
    c, s1, s2 = c_ref[...], s1_ref[...], s2_ref[...]

    def rot(x):
        return x * c + pltpu.roll(x, LANES - ROPE_DIM // 2, 1) * s1 + pltpu.roll(x, ROPE_DIM // 2, 1) * s2

    ones = jnp.ones((SUM_ROWS, vt.shape[1]), BF16)
    for hd in range(N_PAIRS):
        lo = hd * LANES
        q_ref[hd] = (rot(qk[:, lo:lo + LANES]) * scale).astype(BF16)
        k_ref[hd] = rot(qk[:, D_MODEL + lo:D_MODEL + lo + LANES]).astype(BF16)
        vt_ref[hd, 0, 0:LANES] = vt[lo:lo + LANES, :].astype(BF16)
        vt_ref[hd, 0, LANES:VT_ROWS] = ones


def _qkv(h, g, w_qk, w_vt, tm, rope_tables=None, rope_period=None):
    n = h.shape[0]
    nt = n // tm
    tile = pl.BlockSpec((tm, D_MODEL), lambda i: (i, 0))
    in_specs = [tile, _resident((1, D_MODEL)), _resident(w_qk.shape), _resident(w_vt.shape)]
    args = [h, g, w_qk, w_vt]
    if rope_tables is None:
        out_shape = (jax.ShapeDtypeStruct((n, D_MODEL), BF16), jax.ShapeDtypeStruct((n, D_MODEL), BF16),
                     jax.ShapeDtypeStruct((D_MODEL, n), BF16))
        out_specs = (tile, tile, pl.BlockSpec((D_MODEL, tm), lambda i: (0, i)))
    else:
        tab = pl.BlockSpec((tm, LANES), lambda i: (i % rope_period, 0))
        in_specs += [tab, tab, tab]
        args += list(rope_tables)
        hm = pl.BlockSpec((N_PAIRS, tm, LANES), lambda i: (0, i, 0))
        out_shape = (jax.ShapeDtypeStruct((N_PAIRS, n, LANES), BF16), jax.ShapeDtypeStruct((N_PAIRS, n, LANES), BF16),
                     jax.ShapeDtypeStruct((N_PAIRS, nt, VT_ROWS, tm), BF16))
        out_specs = (hm, hm, pl.BlockSpec((N_PAIRS, 1, VT_ROWS, tm), lambda i: (0, i, 0, 0)))
    return pl.pallas_call(
        functools.partial(_qkv_kernel, rope=rope_tables is not None),
        out_shape=out_shape,
        grid=(nt,),
        in_specs=in_specs,
        out_specs=out_specs,
        compiler_params=_params("parallel"),
        name="qkv_rope" if rope_tables is not None else "qkv",
    )(*args)


def _na_kernel(q_ref, k0_ref, k1_ref, k2_ref, v0_ref, v1_ref, v2_ref, qm_ref, km_ref, vtm_ref,
               bias_ref, mb_ref, o_ref, om_ref, kw_ref, vtw_ref):
    rb = pl.program_id(1)
    kw_ref[0:NA_TQ] = k0_ref[...]
    kw_ref[NA_TQ:2 * NA_TQ] = k1_ref[...]
    kw_ref[2 * NA_TQ:NA_TKL] = k2_ref[...]
    kw_ref[NA_TKL:NA_TK] = jnp.zeros((LANES, D_MODEL), BF16)
    kw_ref[NA_TKL:NA_TKL + N_META] = km_ref[...]
    for p in range(N_PAIRS):
        lo = p * LANES
        vtw_ref[p, 0:LANES, 0:NA_TQ] = v0_ref[lo:lo + LANES, :]
        vtw_ref[p, 0:LANES, NA_TQ:2 * NA_TQ] = v1_ref[lo:lo + LANES, :]
        vtw_ref[p, 0:LANES, 2 * NA_TQ:NA_TKL] = v2_ref[lo:lo + LANES, :]
        vtw_ref[p, 0:LANES, NA_TKL:NA_TK] = jnp.zeros((LANES, LANES), BF16)
        vtw_ref[p, 0:LANES, NA_TKL:NA_TKL + N_META] = vtm_ref[0, lo:lo + LANES, :]
        vtw_ref[p, LANES:VT_ROWS, :] = jnp.ones((SUM_ROWS, NA_TK), BF16)

    row = lax.broadcasted_iota(jnp.int32, (LANES, 1), 0)

    def scores(p):
        lo = p * LANES
        qa = _split_maps(q_ref[:, lo:lo + LANES])
        return _dot_nt(kw_ref[:, lo:lo + LANES], qa) + bias_ref[0, p]

    s_next = scores(0)
    for p in range(N_PAIRS):
        lo = p * LANES
        s = s_next
        if p + 1 < N_PAIRS:
            s_next = scores(p + 1)
        e = jnp.exp2(s - jnp.max(s, axis=0, keepdims=True))
        ot = _dot(vtw_ref[p], e.astype(BF16))
        ot = ot[0:LANES] / ot[LANES:LANES + 1]
        o = jnp.where(row < NA_HEAD_DIM, ot[:, :NA_TQ], ot[:, NA_TQ:])
        o_ref[:, lo:lo + LANES] = o.T.astype(BF16)

    @pl.when(rb == 0)
    def _meta_queries():
        first = lax.broadcasted_iota(jnp.int32, (1, LANES), 1) < NA_HEAD_DIM
        for p in range(N_PAIRS):
            lo = p * LANES
            qa = _split_maps(qm_ref[:, lo:lo + LANES])
            kmp = km_ref[:, lo:lo + LANES]
            vtp = vtm_ref[0, lo:lo + LANES, :]
            outs = []
            for half in range(2):
                hd = 2 * p + half
                s = _dot_nt(qa[half * N_META:(half + 1) * N_META], kmp) + mb_ref[hd:hd + 1, :]
                e = jnp.exp2(s - jnp.max(s, axis=-1, keepdims=True))
                pr = e / jnp.sum(e, axis=-1, keepdims=True)
                outs.append(_dot_nt(pr.astype(BF16), vtp))
            om_ref[:, lo:lo + LANES] = jnp.where(first, outs[0], outs[1]).astype(BF16)


def _na_bias(rpb, meta_bias):
    i = jnp.arange(NA_RB)[:, None]
    j = jnp.arange(NA_KR)[None, :]
    dr = jnp.stack([j - i, j - i - WIN_ROWS // 2, j - i - WIN_ROWS])
    row_ok = jnp.stack([jnp.broadcast_to(j < WIN_ROWS, (NA_RB, NA_KR)),
                        (j - i >= 0) & (j - i < WIN_ROWS),
                        jnp.broadcast_to(j >= NA_KR - WIN_ROWS, (NA_RB, NA_KR))])
    qc = jnp.arange(GRID_W)[:, None]
    kc = jnp.arange(GRID_W)[None, :]
    c0 = jnp.clip(qc - WIN_COLS // 2, 0, GRID_W - WIN_COLS)
    col_ok = (kc >= c0) & (kc < c0 + WIN_COLS)
    sel_r = jax.nn.one_hot(dr + WIN_ROWS - 1, 2 * WIN_ROWS - 1, dtype=F32)
    sel_c = jax.nn.one_hot(kc - qc + WIN_COLS - 1, 2 * WIN_COLS - 1, dtype=F32)
    r4 = rpb.astype(F32).reshape(N_PAIRS, 2, 2 * WIN_ROWS - 1, 2 * WIN_COLS - 1)
    vals = jnp.einsum("peab,cija,qkb->cpjkeiq", r4, sel_r, sel_c, precision=lax.Precision.HIGHEST)
    ok = (row_ok.transpose(0, 2, 1)[:, None, :, None, None, :, None]
          & col_ok.T[None, None, None, :, None, None, :])
    loc = jnp.where(ok, vals * LOG2E, MASKED).reshape(3, N_PAIRS, NA_TKL, 2 * NA_TQ)
    mb = (meta_bias.astype(F32) * LOG2E).reshape(N_PAIRS, 2, N_META).transpose(0, 2, 1)
    meta = jnp.broadcast_to(mb[None, :, :, :, None], (3, N_PAIRS, N_META, 2, NA_TQ)).reshape(3, N_PAIRS, N_META, 2 * NA_TQ)
    pad = jnp.full((3, N_PAIRS, LANES - N_META, 2 * NA_TQ), MASKED, F32)
    return jnp.concatenate([loc, meta, pad], axis=2)


def _na_attention(q, k, vt, qm, km, vtm, bias, mb, batch, seq, boff):
    n = seq // NA_TQ
    assert seq % NA_TQ == 0 and n >= 3

    def w0(rb):
        return jnp.clip(rb - 1, 0, n - 3)

    def kspec(jj):
        return pl.BlockSpec((NA_TQ, D_MODEL), lambda b, rb: (b * n + w0(rb) + jj, 0))

    def vspec(jj):
        return pl.BlockSpec((D_MODEL, NA_TQ), lambda b, rb: (0, b * n + w0(rb) + jj))

    mrow = pl.BlockSpec((N_META, D_MODEL), lambda b, rb: (boff + b, 0))
    cls = lambda b, rb: (jnp.where(rb == 0, 0, jnp.where(rb == n - 1, 2, 1)), 0, 0, 0)
    return pl.pallas_call(
        _na_kernel,
        out_shape=(jax.ShapeDtypeStruct((batch * seq, D_MODEL), BF16),
                   jax.ShapeDtypeStruct((batch * N_META, D_MODEL), BF16)),
        grid=(batch, n),
        in_specs=[pl.BlockSpec((NA_TQ, D_MODEL), lambda b, rb: (b * n + rb, 0)),
                  kspec(0), kspec(1), kspec(2), vspec(0), vspec(1), vspec(2),
                  mrow, mrow, pl.BlockSpec((1, D_MODEL, N_META), lambda b, rb: (boff + b, 0, 0)),
                  pl.BlockSpec((1, N_PAIRS, NA_TK, 2 * NA_TQ), cls, pipeline_mode=pl.Buffered(1)),
                  _resident(mb.shape)],
        out_specs=(pl.BlockSpec((NA_TQ, D_MODEL), lambda b, rb: (b * n + rb, 0)),
                   pl.BlockSpec((N_META, D_MODEL), lambda b, rb: (b, 0))),
        scratch_shapes=[pltpu.VMEM((NA_TK, D_MODEL), BF16), pltpu.VMEM((N_PAIRS, VT_ROWS, NA_TK), BF16)],
        compiler_params=_params("parallel", "arbitrary"),
        name="na_attention",
    )(q, k, k, k, vt, vt, vt, qm, km, vtm, bias, mb)


def _diff_kernel(q_ref, k_ref, vt_ref, km_ref, vtm_ref, lam_ref, subln_ref, o_ref, acc_ref, s_ref, *, lambda_init):
    tq = q_ref.shape[1]
    nkv = vt_ref.shape[1]
    tk = vt_ref.shape[3]
    qa = _split_maps(q_ref[0])

    def scores(kv):
        return _dot_nt(k_ref[0, pl.ds(pl.multiple_of(kv * tk, tk), tk), :], qa)

    s = _dot_nt(km_ref[0], qa)
    m0 = jnp.max(s, axis=0, keepdims=True)
    acc_ref[...] = _dot(vtm_ref[0, 0], jnp.exp2(s - m0).astype(BF16))
    s_ref[0] = scores(0)

    def step(kv, m, cur, prefetch):
        if prefetch:
            s_ref[1 - cur] = scores(kv + 1)
        s = s_ref[cur]
        m_new = jnp.maximum(m, jnp.max(s, axis=0, keepdims=True))
        e = jnp.exp2(s - m_new).astype(BF16)
        acc_ref[...] = jnp.exp2(m - m_new) * acc_ref[...] + _dot(vt_ref[0, kv], e)
        return m_new

    unroll = math.gcd(nkv, DIFF_UNROLL)

    def steps(first, m, last):
        for u in range(unroll):
            m = step(first + u, m, u % 2, not (last and u == unroll - 1))
        return m

    m = lax.fori_loop(0, nkv // unroll - 1, lambda i, m: steps(i * unroll, m, False), m0)
    steps(nkv - unroll, m, True)

    lf = lam_ref[...]
    lam = (jnp.exp(jnp.sum(lf[0:1] * lf[1:2], axis=-1, keepdims=True))
           - jnp.exp(jnp.sum(lf[2:3] * lf[3:4], axis=-1, keepdims=True)) + lambda_init)
    on = acc_ref[0:LANES] / acc_ref[LANES:LANES + 1]
    ot = on[:, :tq] - lam * on[:, tq:]
    ms = jnp.mean(ot * ot, axis=0, keepdims=True)
    y = ot * lax.rsqrt(ms + RMS_EPS) * subln_ref[...] * (1.0 - lambda_init)
    o_ref[...] = y.T.astype(BF16)


def _diff_attention(q, k, vt, km, vtm, lam, subln, batch, seq, boff, tq, nq, q_boff, lambda_init):
    nkv = seq // DIFF_TK
    assert nkv % 2 == 0
    return pl.pallas_call(
        functools.partial(_diff_kernel, lambda_init=lambda_init),
        out_shape=jax.ShapeDtypeStruct((batch * nq * tq, D_MODEL), BF16),
        grid=(batch, DIFF_HEADS, nq),
        in_specs=[pl.BlockSpec((1, tq, LANES), lambda b, h, qi: (h, (q_boff + b) * nq + qi, 0)),
                  pl.BlockSpec((1, seq, LANES), lambda b, h, qi: (h, b, 0)),
                  pl.BlockSpec((1, nkv, VT_ROWS, DIFF_TK), lambda b, h, qi: (h, b, 0, 0)),
                  pl.BlockSpec((1, N_META, LANES), lambda b, h, qi: (h, boff + b, 0)),
                  pl.BlockSpec((1, 1, VT_ROWS, N_META), lambda b, h, qi: (boff + b, h, 0, 0)),
                  _resident(lam.shape), _resident(subln.shape)],
        out_specs=pl.BlockSpec((tq, LANES), lambda b, h, qi: (b * nq + qi, h)),
        scratch_shapes=[pltpu.VMEM((VT_ROWS, 2 * tq), F32), pltpu.VMEM((2, DIFF_TK, 2 * tq), F32)],
        compiler_params=_params("parallel", "parallel", "arbitrary"),
        name="diff_attention",
    )(q, k, vt, km, vtm, lam, subln)


def _rope_tables(pos):
    half = ROPE_DIM // 2
    inv = ROPE_THETA ** (-jnp.arange(0, ROPE_DIM, 2, dtype=F32) / ROPE_DIM)
    ang = pos.astype(F32)[:, None] * inv[None, :]
    cos, sin = jnp.cos(ang), jnp.sin(ang)
    n = pos.shape[0]
    rest = DIFF_HEAD_DIM - ROPE_DIM
    z, zr, one = jnp.zeros((n, half), F32), jnp.zeros((n, rest), F32), jnp.ones((n, rest), F32)
    c = jnp.concatenate([cos, cos, one], axis=1)
    s1 = jnp.concatenate([-sin, z, zr], axis=1)
    s2 = jnp.concatenate([z, sin, zr], axis=1)
    return tuple(jnp.concatenate([t, t], axis=1) for t in (c, s1, s2))


def kernel(x_prompt, x_sample, meta_tokens, norm_g, w_ffn_in, w_ffn_out, w_qkv_a, w_o_a, rpb_a,
           meta_bias_a, w_qkv_b, w_o_b, lambda_b, subln_b):
    xs = (x_prompt, x_sample)
    batches = tuple(x.shape[0] for x in xs)
    seqs = tuple(x.shape[1] for x in xs)
    boffs = (0, batches[0])
    nb = sum(batches)
    depth = norm_g.shape[0]
    for t in seqs:
        assert t % (2 * DIFF_TK) == 0 and t % DIFF_TQ == 0 and t % GRID_W == 0

    hs = [x.reshape(-1, D_MODEL).astype(F32) for x in xs]
    hs.append(jnp.broadcast_to(meta_tokens.astype(F32)[None], (nb, N_META, D_MODEL)).reshape(nb * N_META, D_MODEL))
    tms = (DENSE_TM, DENSE_TM, nb * N_META)

    w_in = w_ffn_in.astype(BF16)
    w_out = w_ffn_out.astype(BF16)
    g = norm_g.astype(F32)[:, :, None, :]

    max_seq = max(seqs)
    real_tabs = _rope_tables(N_META + jnp.arange(max_seq))
    meta_tabs = _rope_tables(jnp.tile(jnp.arange(N_META), nb))

    for i in range(depth):
        j = i // N_MIXERS
        hs = [_ffn(h, g[i, 0], g[i, 1], w_in[i, 0], w_out[i, 0], tm) for h, tm in zip(hs, tms)]
        if i % N_MIXERS == 0:
            w = w_qkv_a[j]
            w_qk = w[:, :2 * D_MODEL].astype(BF16)
            w_vt = w[:, 2 * D_MODEL:].T.astype(BF16)
            w_o = w_o_a[j].astype(BF16)
            qkv = [_qkv(h, g[i, 2], w_qk, w_vt, tm) for h, tm in zip(hs, tms)]
            qm, km, vtm = qkv[2]
            vtm = vtm.reshape(D_MODEL, nb, N_META).transpose(1, 0, 2)
            bias = _na_bias(rpb_a[j], meta_bias_a[j])
            mb = meta_bias_a[j].astype(F32) * LOG2E
            outs = [_na_attention(*qkv[gi], qm, km, vtm, bias, mb, batches[gi], seqs[gi], boffs[gi])
                    for gi in range(2)]
            attn = [outs[0][0], outs[1][0], jnp.concatenate([outs[0][1], outs[1][1]], axis=0)]
        else:
            lambda_init = 0.8 - 0.6 * math.exp(-0.3 * i)
            w = w_qkv_b[j]
            w_qk = w[:, :2 * D_MODEL].astype(BF16)
            w_vt = w[:, 2 * D_MODEL:].T.astype(BF16)
            w_o = w_o_b[j].astype(BF16)
            qkv = [_qkv(hs[gi], g[i, 2], w_qk, w_vt, DENSE_TM, real_tabs, seqs[gi] // DENSE_TM) for gi in range(2)]
            qm, km, vtm = _qkv(hs[2], g[i, 2], w_qk, w_vt, tms[2], meta_tabs, 1)
            vtm = vtm.reshape(N_PAIRS, VT_ROWS, nb, N_META).transpose(2, 0, 1, 3)
            qm_pad = jnp.pad(qm.reshape(N_PAIRS, nb, N_META, LANES),
                             ((0, 0), (0, 0), (0, DIFF_TQ_META - N_META), (0, 0)))
            qm_pad = qm_pad.reshape(N_PAIRS, nb * DIFF_TQ_META, LANES)
            lam = lambda_b[j].astype(F32)
            subln = subln_b[j].astype(F32).reshape(LANES, 1)
            attn, attn_meta = [], []
            for gi in range(2):
                q, k, vt = qkv[gi]
                common = (k, vt, km, vtm, lam, subln, batches[gi], seqs[gi], boffs[gi])
                attn.append(_diff_attention(q, *common, DIFF_TQ, seqs[gi] // DIFF_TQ, 0, lambda_init))
                om = _diff_attention(qm_pad, *common, DIFF_TQ_META, 1, boffs[gi], lambda_init)
                attn_meta.append(om.reshape(batches[gi], DIFF_TQ_META, D_MODEL)[:, :N_META].reshape(-1, D_MODEL))
            attn.append(jnp.concatenate(attn_meta, axis=0))
        hs = [_mix_ffn(h, a, w_o, g[i, 3], g[i, 4], g[i, 5], w_in[i, 1], w_out[i, 1], tm)
              for h, a, tm in zip(hs, attn, tms)]

    return tuple(h.reshape(x.shape).astype(x.dtype) for h, x in zip(hs[:2], xs))
```

```python
import functools
import math

import jax
import jax.numpy as jnp
from jax import lax
from jax.experimental import pallas as pl
from jax.experimental.pallas import tpu as pltpu

F32 = jnp.float32
BF16 = jnp.bfloat16

D_MODEL = 1024
N_META = 16
GRID_W = 64
WIN_ROWS = 8
WIN_COLS = 16
NA_HEADS = 16
NA_HEAD_DIM = D_MODEL // NA_HEADS
DIFF_HEADS = 8
DIFF_HEAD_DIM = D_MODEL // (2 * DIFF_HEADS)
ROPE_THETA = 500000.0
ROPE_DIM = DIFF_HEAD_DIM // 4
RMS_EPS = 1e-6
N_MIXERS = 2

LANES = 128
N_PAIRS = D_MODEL // LANES
VMEM_LIMIT = 56 * 1024 * 1024
MASKED = -1e30
LOG2E = 1.4426950408889634
SUM_ROWS = 16
VT_ROWS = LANES + SUM_ROWS

DENSE_TM = 512

NA_RB = 4
NA_KR = NA_RB + WIN_ROWS
NA_TQ = NA_RB * GRID_W
NA_TKL = NA_KR * GRID_W
NA_TK = NA_TKL + LANES

DIFF_TQ = 512
DIFF_TK = DENSE_TM
DIFF_TQ_META = 128


def _rms(x, g):
    ms = jnp.mean(x * x, axis=-1, keepdims=True)
    return x * lax.rsqrt(ms + RMS_EPS) * g


def _dot(a, b):
    return jnp.dot(a, b, preferred_element_type=F32)


def _dot_nt(a, b):
    return lax.dot_general(a, b, (((1,), (1,)), ((), ())), preferred_element_type=F32)


def _resident(shape):
    nd = len(shape)
    return pl.BlockSpec(shape, lambda *_: (0,) * nd, pipeline_mode=pl.Buffered(1))


def _params(*sem):
    return pltpu.CompilerParams(dimension_semantics=sem, vmem_limit_bytes=VMEM_LIMIT)


def _split_maps(x):
    first = lax.broadcasted_iota(jnp.int32, (1, LANES), 1) < LANES // 2
    zero = jnp.zeros((), x.dtype)
    return jnp.concatenate([jnp.where(first, x, zero), jnp.where(first, zero, x)], axis=0)


def _half_ffn(x, ga, gb, win_ref, wout_ref):
    xn = _rms(x, ga).astype(BF16)
    gu = _dot(xn, win_ref[...])
    d_ff = wout_ref.shape[0]
    gate = gu[:, :d_ff]
    act = (gate * jax.nn.sigmoid(gate) * gu[:, d_ff:]).astype(BF16)
    y = _dot(act, wout_ref[...])
    return x + 0.5 * _rms(y, gb)


def _ffn_kernel(h_ref, ga_ref, gb_ref, win_ref, wout_ref, o_ref):
    o_ref[...] = _half_ffn(h_ref[...], ga_ref[...], gb_ref[...], win_ref, wout_ref)


def _ffn(h, ga, gb, w_in, w_out, tm):
    n = h.shape[0]
    tile = pl.BlockSpec((tm, D_MODEL), lambda i: (i, 0))
    return pl.pallas_call(
        _ffn_kernel,
        out_shape=jax.ShapeDtypeStruct((n, D_MODEL), F32),
        grid=(n // tm,),
        in_specs=[tile, _resident((1, D_MODEL)), _resident((1, D_MODEL)),
                  _resident(w_in.shape), _resident(w_out.shape)],
        out_specs=tile,
        compiler_params=_params("parallel"),
        name="ffn",
    )(h, ga, gb, w_in, w_out)


def _mix_ffn_kernel(h_ref, a_ref, wo_ref, gm_ref, ga_ref, gb_ref, win_ref, wout_ref, o_ref):
    x = h_ref[...] + _rms(_dot(a_ref[...], wo_ref[...]), gm_ref[...])
    o_ref[...] = _half_ffn(x, ga_ref[...], gb_ref[...], win_ref, wout_ref)


def _mix_ffn(h, a, w_o, gm, ga, gb, w_in, w_out, tm):
    n = h.shape[0]
    tile = pl.BlockSpec((tm, D_MODEL), lambda i: (i, 0))
    vec = _resident((1, D_MODEL))
    return pl.pallas_call(
        _mix_ffn_kernel,
        out_shape=jax.ShapeDtypeStruct((n, D_MODEL), F32),
        grid=(n // tm,),
        in_specs=[tile, tile, _resident(w_o.shape), vec, vec, vec, _resident(w_in.shape), _resident(w_out.shape)],
        out_specs=tile,
        compiler_params=_params("parallel"),
        name="mix_ffn",
    )(h, a, w_o, gm, ga, gb, w_in, w_out)


def _qkv_kernel(*refs, rope):
    if rope:
        h_ref, g_ref, wqk_ref, wvt_ref, c_ref, s1_ref, s2_ref, q_ref, k_ref, vt_ref = refs
    else:
        h_ref, g_ref, wqk_ref, wvt_ref, q_ref, k_ref, vt_ref = refs
    xn = _rms(h_ref[...], g_ref[...]).astype(BF16)
    qk = _dot(xn, wqk_ref[...])
    vt = _dot_nt(wvt_ref[...], xn)
    scale = NA_HEAD_DIM ** -0.5 * LOG2E
    if not rope:
        q_ref[...] = (qk[:, :D_MODEL] * scale).astype(BF16)
        k_ref[...] = qk[:, D_MODEL:].astype(BF16)
        vt_ref[...] = vt.astype(BF16)
        return
    c, s1, s2 = c_ref[...], s1_ref[...], s2_ref[...]

    def rot(x):
        return x * c + pltpu.roll(x, LANES - ROPE_DIM // 2, 1) * s1 + pltpu.roll(x, ROPE_DIM // 2, 1) * s2

    ones = jnp.ones((SUM_ROWS, vt.shape[1]), BF16)
    for hd in range(N_PAIRS):
        lo = hd * LANES
        q_ref[hd] = (rot(qk[:, lo:lo + LANES]) * scale).astype(BF16)
        k_ref[hd] = rot(qk[:, D_MODEL + lo:D_MODEL + lo + LANES]).astype(BF16)
        vt_ref[hd, 0, 0:LANES] = vt[lo:lo + LANES, :].astype(BF16)
        vt_ref[hd, 0, LANES:VT_ROWS] = ones


def _qkv(h, g, w_qk, w_vt, tm, rope_tables=None, rope_period=None):
    n = h.shape[0]
    nt = n // tm
    tile = pl.BlockSpec((tm, D_MODEL), lambda i: (i, 0))
    in_specs = [tile, _resident((1, D_MODEL)), _resident(w_qk.shape), _resident(w_vt.shape)]
    args = [h, g, w_qk, w_vt]
    if rope_tables is None:
        out_shape = (jax.ShapeDtypeStruct((n, D_MODEL), BF16), jax.ShapeDtypeStruct((n, D_MODEL), BF16),
                     jax.ShapeDtypeStruct((D_MODEL, n), BF16))
        out_specs = (tile, tile, pl.BlockSpec((D_MODEL, tm), lambda i: (0, i)))
    else:
        tab = pl.BlockSpec((tm, LANES), lambda i: (i % rope_period, 0))
        in_specs += [tab, tab, tab]
        args += list(rope_tables)
        hm = pl.BlockSpec((N_PAIRS, tm, LANES), lambda i: (0, i, 0))
        out_shape = (jax.ShapeDtypeStruct((N_PAIRS, n, LANES), BF16), jax.ShapeDtypeStruct((N_PAIRS, n, LANES), BF16),
                     jax.ShapeDtypeStruct((N_PAIRS, nt, VT_ROWS, tm), BF16))
        out_specs = (hm, hm, pl.BlockSpec((N_PAIRS, 1, VT_ROWS, tm), lambda i: (0, i, 0, 0)))
    return pl.pallas_call(
        functools.partial(_qkv_kernel, rope=rope_tables is not None),
        out_shape=out_shape,
        grid=(nt,),
        in_specs=in_specs,
        out_specs=out_specs,
        compiler_params=_params("parallel"),
        name="qkv_rope" if rope_tables is not None else "qkv",
    )(*args)


def _na_kernel(q_ref, k0_ref, k1_ref, k2_ref, v0_ref, v1_ref, v2_ref, qm_ref, km_ref, vtm_ref,
               bias_ref, mb_ref, o_ref, om_ref, kw_ref, vtw_ref):
    rb = pl.program_id(1)
    kw_ref[0:NA_TQ] = k0_ref[...]
    kw_ref[NA_TQ:2 * NA_TQ] = k1_ref[...]
    kw_ref[2 * NA_TQ:NA_TKL] = k2_ref[...]
    kw_ref[NA_TKL:NA_TK] = jnp.zeros((LANES, D_MODEL), BF16)
    kw_ref[NA_TKL:NA_TKL + N_META] = km_ref[...]
    for p in range(N_PAIRS):
        lo = p * LANES
        vtw_ref[p, 0:LANES, 0:NA_TQ] = v0_ref[lo:lo + LANES, :]
        vtw_ref[p, 0:LANES, NA_TQ:2 * NA_TQ] = v1_ref[lo:lo + LANES, :]
        vtw_ref[p, 0:LANES, 2 * NA_TQ:NA_TKL] = v2_ref[lo:lo + LANES, :]
        vtw_ref[p, 0:LANES, NA_TKL:NA_TK] = jnp.zeros((LANES, LANES), BF16)
        vtw_ref[p, 0:LANES, NA_TKL:NA_TKL + N_META] = vtm_ref[0, lo:lo + LANES, :]
        vtw_ref[p, LANES:VT_ROWS, :] = jnp.ones((SUM_ROWS, NA_TK), BF16)

    row = lax.broadcasted_iota(jnp.int32, (LANES, 1), 0)

    def scores(p):
        lo = p * LANES
        qa = _split_maps(q_ref[:, lo:lo + LANES])
        return _dot_nt(kw_ref[:, lo:lo + LANES], qa) + bias_ref[0, p]

    s_next = scores(0)
    for p in range(N_PAIRS):
        lo = p * LANES
        s = s_next
        if p + 1 < N_PAIRS:
            s_next = scores(p + 1)
        e = jnp.exp2(s - jnp.max(s, axis=0, keepdims=True))
        ot = _dot(vtw_ref[p], e.astype(BF16))
        ot = ot[0:LANES] / ot[LANES:LANES + 1]
        o = jnp.where(row < NA_HEAD_DIM, ot[:, :NA_TQ], ot[:, NA_TQ:])
        o_ref[:, lo:lo + LANES] = o.T.astype(BF16)

    @pl.when(rb == 0)
    def _meta_queries():
        first = lax.broadcasted_iota(jnp.int32, (1, LANES), 1) < NA_HEAD_DIM
        for p in range(N_PAIRS):
            lo = p * LANES
            qa = _split_maps(qm_ref[:, lo:lo + LANES])
            kmp = km_ref[:, lo:lo + LANES]
            vtp = vtm_ref[0, lo:lo + LANES, :]
            outs = []
            for half in range(2):
                hd = 2 * p + half
                s = _dot_nt(qa[half * N_META:(half + 1) * N_META], kmp) + mb_ref[hd:hd + 1, :]
                e = jnp.exp2(s - jnp.max(s, axis=-1, keepdims=True))
                pr = e / jnp.sum(e, axis=-1, keepdims=True)
                outs.append(_dot_nt(pr.astype(BF16), vtp))
            om_ref[:, lo:lo + LANES] = jnp.where(first, outs[0], outs[1]).astype(BF16)


def _na_bias(rpb, meta_bias):
    i = jnp.arange(NA_RB)[:, None]
    j = jnp.arange(NA_KR)[None, :]
    dr = jnp.stack([j - i, j - i - WIN_ROWS // 2, j - i - WIN_ROWS])
    row_ok = jnp.stack([jnp.broadcast_to(j < WIN_ROWS, (NA_RB, NA_KR)),
                        (j - i >= 0) & (j - i < WIN_ROWS),
                        jnp.broadcast_to(j >= NA_KR - WIN_ROWS, (NA_RB, NA_KR))])
    qc = jnp.arange(GRID_W)[:, None]
    kc = jnp.arange(GRID_W)[None, :]
    c0 = jnp.clip(qc - WIN_COLS // 2, 0, GRID_W - WIN_COLS)
    col_ok = (kc >= c0) & (kc < c0 + WIN_COLS)
    sel_r = jax.nn.one_hot(dr + WIN_ROWS - 1, 2 * WIN_ROWS - 1, dtype=F32)
    sel_c = jax.nn.one_hot(kc - qc + WIN_COLS - 1, 2 * WIN_COLS - 1, dtype=F32)
    r4 = rpb.astype(F32).reshape(N_PAIRS, 2, 2 * WIN_ROWS - 1, 2 * WIN_COLS - 1)
    vals = jnp.einsum("peab,cija,qkb->cpjkeiq", r4, sel_r, sel_c, precision=lax.Precision.HIGHEST)
    ok = (row_ok.transpose(0, 2, 1)[:, None, :, None, None, :, None]
          & col_ok.T[None, None, None, :, None, None, :])
    loc = jnp.where(ok, vals * LOG2E, MASKED).reshape(3, N_PAIRS, NA_TKL, 2 * NA_TQ)
    mb = (meta_bias.astype(F32) * LOG2E).reshape(N_PAIRS, 2, N_META).transpose(0, 2, 1)
    meta = jnp.broadcast_to(mb[None, :, :, :, None], (3, N_PAIRS, N_META, 2, NA_TQ)).reshape(3, N_PAIRS, N_META, 2 * NA_TQ)
    pad = jnp.full((3, N_PAIRS, LANES - N_META, 2 * NA_TQ), MASKED, F32)
    return jnp.concatenate([loc, meta, pad], axis=2)


def _na_attention(q, k, vt, qm, km, vtm, bias, mb, batch, seq, boff):
    n = seq // NA_TQ
    assert seq % NA_TQ == 0 and n >= 3

    def w0(rb):
        return jnp.clip(rb - 1, 0, n - 3)

    def kspec(jj):
        return pl.BlockSpec((NA_TQ, D_MODEL), lambda b, rb: (b * n + w0(rb) + jj, 0))

    def vspec(jj):
        return pl.BlockSpec((D_MODEL, NA_TQ), lambda b, rb: (0, b * n + w0(rb) + jj))

    mrow = pl.BlockSpec((N_META, D_MODEL), lambda b, rb: (boff + b, 0))
    cls = lambda b, rb: (jnp.where(rb == 0, 0, jnp.where(rb == n - 1, 2, 1)), 0, 0, 0)
    return pl.pallas_call(
        _na_kernel,
        out_shape=(jax.ShapeDtypeStruct((batch * seq, D_MODEL), BF16),
                   jax.ShapeDtypeStruct((batch * N_META, D_MODEL), BF16)),
        grid=(batch, n),
        in_specs=[pl.BlockSpec((NA_TQ, D_MODEL), lambda b, rb: (b * n + rb, 0)),
                  kspec(0), kspec(1), kspec(2), vspec(0), vspec(1), vspec(2),
                  mrow, mrow, pl.BlockSpec((1, D_MODEL, N_META), lambda b, rb: (boff + b, 0, 0)),
                  pl.BlockSpec((1, N_PAIRS, NA_TK, 2 * NA_TQ), cls, pipeline_mode=pl.Buffered(1)),
                  _resident(mb.shape)],
        out_specs=(pl.BlockSpec((NA_TQ, D_MODEL), lambda b, rb: (b * n + rb, 0)),
                   pl.BlockSpec((N_META, D_MODEL), lambda b, rb: (b, 0))),
        scratch_shapes=[pltpu.VMEM((NA_TK, D_MODEL), BF16), pltpu.VMEM((N_PAIRS, VT_ROWS, NA_TK), BF16)],
        compiler_params=_params("parallel", "arbitrary"),
        name="na_attention",
    )(q, k, k, k, vt, vt, vt, qm, km, vtm, bias, mb)


def _diff_kernel(q_ref, k_ref, vt_ref, km_ref, vtm_ref, lam_ref, subln_ref, o_ref, acc_ref, s_ref, *,
                 tq, lambda_init):
    nkv, tk = vt_ref.shape[1], vt_ref.shape[3]
    nq = q_ref.shape[1] // tq

    def split_q(qi):
        return _split_maps(q_ref[0, pl.ds(pl.multiple_of(qi * tq, tq), tq), :])

    def scores(qa, kv):
        return _dot_nt(k_ref[0, kv * tk:(kv + 1) * tk, :], qa)

    def start_tile(qa):
        s = _dot_nt(km_ref[0], qa)
        m = jnp.max(s, axis=0, keepdims=True)
        acc_ref[...] = _dot(vtm_ref[0, 0], jnp.exp2(s - m).astype(BF16))
        return m

    lf = lam_ref[...]
    lam = (jnp.exp(jnp.sum(lf[0:1] * lf[1:2], axis=-1, keepdims=True))
           - jnp.exp(jnp.sum(lf[2:3] * lf[3:4], axis=-1, keepdims=True)) + lambda_init)

    def finish_tile(qi):
        on = acc_ref[0:LANES] / acc_ref[LANES:LANES + 1]
        ot = on[:, :tq] - lam * on[:, tq:]
        ms = jnp.mean(ot * ot, axis=0, keepdims=True)
        y = ot * lax.rsqrt(ms + RMS_EPS) * subln_ref[...] * (1.0 - lambda_init)
        o_ref[pl.ds(pl.multiple_of(qi * tq, tq), tq), :] = y.T.astype(BF16)

    qa0 = split_q(0)
    s_ref[0] = scores(qa0, 0)
    m0 = start_tile(qa0)

    def q_tile(qi, m):
        qa = split_q(qi)
        qa_next = split_q(jnp.minimum(qi + 1, nq - 1))
        for kv in range(nkv):
            slot = kv % 2
            s_ref[1 - slot] = scores(qa, kv + 1) if kv + 1 < nkv else scores(qa_next, 0)
            s = s_ref[slot]
            m_new = jnp.maximum(m, jnp.max(s, axis=0, keepdims=True))
            e = jnp.exp2(s - m_new).astype(BF16)
            acc_ref[...] = jnp.exp2(m - m_new) * acc_ref[...] + _dot(vt_ref[0, kv], e)
            m = m_new
        finish_tile(qi)
        return start_tile(qa_next)

    lax.fori_loop(0, nq, q_tile, m0)


def _diff_attention(q, k, vt, km, vtm, lam, subln, batch, seq, boff, tq, nq, q_boff, lambda_init):
    nkv = seq // DIFF_TK
    assert nkv % 2 == 0
    return pl.pallas_call(
        functools.partial(_diff_kernel, tq=tq, lambda_init=lambda_init),
        out_shape=jax.ShapeDtypeStruct((batch * nq * tq, D_MODEL), BF16),
        grid=(batch, DIFF_HEADS),
        in_specs=[pl.BlockSpec((1, nq * tq, LANES), lambda b, h: (h, q_boff + b, 0)),
                  pl.BlockSpec((1, seq, LANES), lambda b, h: (h, b, 0)),
                  pl.BlockSpec((1, nkv, VT_ROWS, DIFF_TK), lambda b, h: (h, b, 0, 0)),
                  pl.BlockSpec((1, N_META, LANES), lambda b, h: (h, boff + b, 0)),
                  pl.BlockSpec((1, 1, VT_ROWS, N_META), lambda b, h: (boff + b, h, 0, 0)),
                  _resident(lam.shape), _resident(subln.shape)],
        out_specs=pl.BlockSpec((nq * tq, LANES), lambda b, h: (b, h)),
        scratch_shapes=[pltpu.VMEM((VT_ROWS, 2 * tq), F32), pltpu.VMEM((2, DIFF_TK, 2 * tq), F32)],
        compiler_params=_params("parallel", "parallel"),
        name="diff_attention",
    )(q, k, vt, km, vtm, lam, subln)


def _rope_tables(pos):
    half = ROPE_DIM // 2
    inv = ROPE_THETA ** (-jnp.arange(0, ROPE_DIM, 2, dtype=F32) / ROPE_DIM)
    ang = pos.astype(F32)[:, None] * inv[None, :]
    cos, sin = jnp.cos(ang), jnp.sin(ang)
    n = pos.shape[0]
    rest = DIFF_HEAD_DIM - ROPE_DIM
    z, zr, one = jnp.zeros((n, half), F32), jnp.zeros((n, rest), F32), jnp.ones((n, rest), F32)
    c = jnp.concatenate([cos, cos, one], axis=1)
    s1 = jnp.concatenate([-sin, z, zr], axis=1)
    s2 = jnp.concatenate([z, sin, zr], axis=1)
    return tuple(jnp.concatenate([t, t], axis=1) for t in (c, s1, s2))


def kernel(x_prompt, x_sample, meta_tokens, norm_g, w_ffn_in, w_ffn_out, w_qkv_a, w_o_a, rpb_a,
           meta_bias_a, w_qkv_b, w_o_b, lambda_b, subln_b):
    xs = (x_prompt, x_sample)
    batches = tuple(x.shape[0] for x in xs)
    seqs = tuple(x.shape[1] for x in xs)
    boffs = (0, batches[0])
    nb = sum(batches)
    depth = norm_g.shape[0]
    for t in seqs:
        assert t % (2 * DIFF_TK) == 0 and t % DIFF_TQ == 0 and t % GRID_W == 0

    hs = [x.reshape(-1, D_MODEL).astype(F32) for x in xs]
    hs.append(jnp.broadcast_to(meta_tokens.astype(F32)[None], (nb, N_META, D_MODEL)).reshape(nb * N_META, D_MODEL))
    tms = (DENSE_TM, DENSE_TM, nb * N_META)

    w_in = w_ffn_in.astype(BF16)
    w_out = w_ffn_out.astype(BF16)
    g = norm_g.astype(F32)[:, :, None, :]

    max_seq = max(seqs)
    real_tabs = _rope_tables(N_META + jnp.arange(max_seq))
    meta_tabs = _rope_tables(jnp.tile(jnp.arange(N_META), nb))

    for i in range(depth):
        j = i // N_MIXERS
        hs = [_ffn(h, g[i, 0], g[i, 1], w_in[i, 0], w_out[i, 0], tm) for h, tm in zip(hs, tms)]
        if i % N_MIXERS == 0:
            w = w_qkv_a[j]
            w_qk = w[:, :2 * D_MODEL].astype(BF16)
            w_vt = w[:, 2 * D_MODEL:].T.astype(BF16)
            w_o = w_o_a[j].astype(BF16)
            qkv = [_qkv(h, g[i, 2], w_qk, w_vt, tm) for h, tm in zip(hs, tms)]
            qm, km, vtm = qkv[2]
            vtm = vtm.reshape(D_MODEL, nb, N_META).transpose(1, 0, 2)
            bias = _na_bias(rpb_a[j], meta_bias_a[j])
            mb = meta_bias_a[j].astype(F32) * LOG2E
            outs = [_na_attention(*qkv[gi], qm, km, vtm, bias, mb, batches[gi], seqs[gi], boffs[gi])
                    for gi in range(2)]
            attn = [outs[0][0], outs[1][0], jnp.concatenate([outs[0][1], outs[1][1]], axis=0)]
        else:
            lambda_init = 0.8 - 0.6 * math.exp(-0.3 * i)
            w = w_qkv_b[j]
            w_qk = w[:, :2 * D_MODEL].astype(BF16)
            w_vt = w[:, 2 * D_MODEL:].T.astype(BF16)
            w_o = w_o_b[j].astype(BF16)
            qkv = [_qkv(hs[gi], g[i, 2], w_qk, w_vt, DENSE_TM, real_tabs, seqs[gi] // DENSE_TM) for gi in range(2)]
            qm, km, vtm = _qkv(hs[2], g[i, 2], w_qk, w_vt, tms[2], meta_tabs, 1)
            vtm = vtm.reshape(N_PAIRS, VT_ROWS, nb, N_META).transpose(2, 0, 1, 3)
            qm_pad = jnp.pad(qm.reshape(N_PAIRS, nb, N_META, LANES),
                             ((0, 0), (0, 0), (0, DIFF_TQ_META - N_META), (0, 0)))
            qm_pad = qm_pad.reshape(N_PAIRS, nb * DIFF_TQ_META, LANES)
            lam = lambda_b[j].astype(F32)
            subln = subln_b[j].astype(F32).reshape(LANES, 1)
            attn, attn_meta = [], []
            for gi in range(2):
                q, k, vt = qkv[gi]
                common = (k, vt, km, vtm, lam, subln, batches[gi], seqs[gi], boffs[gi])
                attn.append(_diff_attention(q, *common, DIFF_TQ, seqs[gi] // DIFF_TQ, 0, lambda_init))
                om = _diff_attention(qm_pad, *common, DIFF_TQ_META, 1, boffs[gi], lambda_init)
                attn_meta.append(om.reshape(batches[gi], DIFF_TQ_META, D_MODEL)[:, :N_META].reshape(-1, D_MODEL))
            attn.append(jnp.concatenate(attn_meta, axis=0))
        hs = [_mix_ffn(h, a, w_o, g[i, 3], g[i, 4], g[i, 5], w_in[i, 1], w_out[i, 1], tm)
              for h, a, tm in zip(hs, attn, tms)]

    return tuple(h.reshape(x.shape).astype(x.dtype) for h, x in zip(hs[:2], xs))
```

```python
import functools
import math

import jax
import jax.numpy as jnp
from jax import lax
from jax.experimental import pallas as pl
from jax.experimental.pallas import tpu as pltpu

F32 = jnp.float32
BF16 = jnp.bfloat16

D_MODEL = 1024
N_META = 16
GRID_W = 64
WIN_ROWS = 8
WIN_COLS = 16
NA_HEADS = 16
NA_HEAD_DIM = D_MODEL // NA_HEADS
DIFF_HEADS = 8
DIFF_HEAD_DIM = D_MODEL // (2 * DIFF_HEADS)
ROPE_THETA = 500000.0
ROPE_DIM = DIFF_HEAD_DIM // 4
RMS_EPS = 1e-6
N_MIXERS = 2

LANES = 128
N_PAIRS = D_MODEL // LANES
VMEM_LIMIT = 56 * 1024 * 1024
MASKED = -1e30
LOG2E = 1.4426950408889634
SUM_ROWS = 16
VT_ROWS = LANES + SUM_ROWS

DENSE_TM = 512

NA_RB = 4
NA_KR = NA_RB + WIN_ROWS
NA_TQ = NA_RB * GRID_W
NA_TKL = NA_KR * GRID_W
NA_TK = NA_TKL + LANES

DIFF_TQ = 512
DIFF_TK = 512
DIFF_TQ_META = 128


def _rms(x, g):
    ms = jnp.mean(x * x, axis=-1, keepdims=True)
    return x * lax.rsqrt(ms + RMS_EPS) * g


def _dot(a, b):
    return jnp.dot(a, b, preferred_element_type=F32)


def _dot_nt(a, b):
    return lax.dot_general(a, b, (((1,), (1,)), ((), ())), preferred_element_type=F32)


def _resident(shape):
    nd = len(shape)
    return pl.BlockSpec(shape, lambda *_: (0,) * nd, pipeline_mode=pl.Buffered(1))


def _params(*sem):
    return pltpu.CompilerParams(dimension_semantics=sem, vmem_limit_bytes=VMEM_LIMIT)


def _split_maps(x):
    first = lax.broadcasted_iota(jnp.int32, (1, LANES), 1) < LANES // 2
    zero = jnp.zeros((), x.dtype)
    return jnp.concatenate([jnp.where(first, x, zero), jnp.where(first, zero, x)], axis=0)


def _half_ffn(x, ga, gb, win_ref, wout_ref):
    xn = _rms(x, ga).astype(BF16)
    gu = _dot(xn, win_ref[...])
    d_ff = wout_ref.shape[0]
    gate = gu[:, :d_ff]
    act = (gate * jax.nn.sigmoid(gate) * gu[:, d_ff:]).astype(BF16)
    y = _dot(act, wout_ref[...])
    return x + 0.5 * _rms(y, gb)


def _ffn_kernel(h_ref, ga_ref, gb_ref, win_ref, wout_ref, o_ref):
    o_ref[...] = _half_ffn(h_ref[...], ga_ref[...], gb_ref[...], win_ref, wout_ref)


def _ffn(h, ga, gb, w_in, w_out, tm):
    n = h.shape[0]
    tile = pl.BlockSpec((tm, D_MODEL), lambda i: (i, 0))
    return pl.pallas_call(
        _ffn_kernel,
        out_shape=jax.ShapeDtypeStruct((n, D_MODEL), F32),
        grid=(n // tm,),
        in_specs=[tile, _resident((1, D_MODEL)), _resident((1, D_MODEL)),
                  _resident(w_in.shape), _resident(w_out.shape)],
        out_specs=tile,
        compiler_params=_params("parallel"),
        name="ffn",
    )(h, ga, gb, w_in, w_out)


def _mix_ffn_kernel(h_ref, a_ref, wo_ref, gm_ref, ga_ref, gb_ref, win_ref, wout_ref, o_ref):
    x = h_ref[...] + _rms(_dot(a_ref[...], wo_ref[...]), gm_ref[...])
    o_ref[...] = _half_ffn(x, ga_ref[...], gb_ref[...], win_ref, wout_ref)


def _mix_ffn(h, a, w_o, gm, ga, gb, w_in, w_out, tm):
    n = h.shape[0]
    tile = pl.BlockSpec((tm, D_MODEL), lambda i: (i, 0))
    vec = _resident((1, D_MODEL))
    return pl.pallas_call(
        _mix_ffn_kernel,
        out_shape=jax.ShapeDtypeStruct((n, D_MODEL), F32),
        grid=(n // tm,),
        in_specs=[tile, tile, _resident(w_o.shape), vec, vec, vec, _resident(w_in.shape), _resident(w_out.shape)],
        out_specs=tile,
        compiler_params=_params("parallel"),
        name="mix_ffn",
    )(h, a, w_o, gm, ga, gb, w_in, w_out)


def _qkv_kernel(*refs, rope):
    if rope:
        h_ref, g_ref, wqk_ref, wvt_ref, c_ref, s1_ref, s2_ref, q_ref, k_ref, vt_ref = refs
    else:
        h_ref, g_ref, wqk_ref, wvt_ref, q_ref, k_ref, vt_ref = refs
    xn = _rms(h_ref[...], g_ref[...]).astype(BF16)
    qk = _dot(xn, wqk_ref[...])
    vt = _dot_nt(wvt_ref[...], xn)
    scale = NA_HEAD_DIM ** -0.5 * LOG2E
    if not rope:
        q_ref[...] = (qk[:, :D_MODEL] * scale).astype(BF16)
        k_ref[...] = qk[:, D_MODEL:].astype(BF16)
        if len(vt_ref.shape) == 2:
            vt_ref[...] = vt.astype(BF16)
        else:
            blk = vt_ref.shape[2]
            for c in range(vt_ref.shape[0]):
                vt_ref[c] = vt[:, c * blk:(c + 1) * blk].astype(BF16)
        return
    c, s1, s2 = c_ref[...], s1_ref[...], s2_ref[...]

    def rot(x):
        return x * c + pltpu.roll(x, LANES - ROPE_DIM // 2, 1) * s1 + pltpu.roll(x, ROPE_DIM // 2, 1) * s2

    ones = jnp.ones((SUM_ROWS, vt.shape[1]), BF16)
    for hd in range(N_PAIRS):
        lo = hd * LANES
        q_ref[hd] = (rot(qk[:, lo:lo + LANES]) * scale).astype(BF16)
        k_ref[hd] = rot(qk[:, D_MODEL + lo:D_MODEL + lo + LANES]).astype(BF16)
        vt_ref[hd, 0, 0:LANES] = vt[lo:lo + LANES, :].astype(BF16)
        vt_ref[hd, 0, LANES:VT_ROWS] = ones


def _qkv(h, g, w_qk, w_vt, tm, rope_tables=None, rope_period=None, vt_block=None):
    n = h.shape[0]
    nt = n // tm
    tile = pl.BlockSpec((tm, D_MODEL), lambda i: (i, 0))
    in_specs = [tile, _resident((1, D_MODEL)), _resident(w_qk.shape), _resident(w_vt.shape)]
    args = [h, g, w_qk, w_vt]
    if rope_tables is None:
        if vt_block is None:
            vt_shape, vt_spec = (D_MODEL, n), pl.BlockSpec((D_MODEL, tm), lambda i: (0, i))
        else:
            vt_shape = (n // vt_block, D_MODEL, vt_block)
            vt_spec = pl.BlockSpec((tm // vt_block, D_MODEL, vt_block), lambda i: (i, 0, 0))
        out_shape = (jax.ShapeDtypeStruct((n, D_MODEL), BF16), jax.ShapeDtypeStruct((n, D_MODEL), BF16),
                     jax.ShapeDtypeStruct(vt_shape, BF16))
        out_specs = (tile, tile, vt_spec)
    else:
        tab = pl.BlockSpec((tm, LANES), lambda i: (i % rope_period, 0))
        in_specs += [tab, tab, tab]
        args += list(rope_tables)
        hm = pl.BlockSpec((N_PAIRS, tm, LANES), lambda i: (0, i, 0))
        out_shape = (jax.ShapeDtypeStruct((N_PAIRS, n, LANES), BF16), jax.ShapeDtypeStruct((N_PAIRS, n, LANES), BF16),
                     jax.ShapeDtypeStruct((N_PAIRS, nt, VT_ROWS, tm), BF16))
        out_specs = (hm, hm, pl.BlockSpec((N_PAIRS, 1, VT_ROWS, tm), lambda i: (0, i, 0, 0)))
    return pl.pallas_call(
        functools.partial(_qkv_kernel, rope=rope_tables is not None),
        out_shape=out_shape,
        grid=(nt,),
        in_specs=in_specs,
        out_specs=out_specs,
        compiler_params=_params("parallel"),
        name="qkv_rope" if rope_tables is not None else "qkv",
    )(*args)


def _na_kernel(q_ref, k0_ref, k1_ref, k2_ref, v0_ref, v1_ref, v2_ref, km_ref, vtm_ref, bias_ref, o_ref,
               kw_ref, vtw_ref):
    kw_ref[0:NA_TQ] = k0_ref[...]
    kw_ref[NA_TQ:2 * NA_TQ] = k1_ref[...]
    kw_ref[2 * NA_TQ:NA_TKL] = k2_ref[...]
    kw_ref[NA_TKL:NA_TK] = jnp.zeros((LANES, D_MODEL), BF16)
    kw_ref[NA_TKL:NA_TKL + N_META] = km_ref[...]
    for p in range(N_PAIRS):
        lo = p * LANES
        vtw_ref[p, 0:LANES, 0:NA_TQ] = v0_ref[0, lo:lo + LANES, :]
        vtw_ref[p, 0:LANES, NA_TQ:2 * NA_TQ] = v1_ref[0, lo:lo + LANES, :]
        vtw_ref[p, 0:LANES, 2 * NA_TQ:NA_TKL] = v2_ref[0, lo:lo + LANES, :]
        vtw_ref[p, 0:LANES, NA_TKL:NA_TK] = jnp.zeros((LANES, LANES), BF16)
        vtw_ref[p, 0:LANES, NA_TKL:NA_TKL + N_META] = vtm_ref[0, lo:lo + LANES, :]
        vtw_ref[p, LANES:VT_ROWS, :] = jnp.ones((SUM_ROWS, NA_TK), BF16)

    row = lax.broadcasted_iota(jnp.int32, (LANES, 1), 0)

    def scores(p):
        lo = p * LANES
        qa = _split_maps(q_ref[:, lo:lo + LANES])
        return _dot_nt(kw_ref[:, lo:lo + LANES], qa) + bias_ref[0, p]

    def softmax(s):
        return jnp.exp2(s - jnp.max(s, axis=0, keepdims=True)).astype(BF16)

    s_first = scores(0)
    s_next = scores(1)
    e = softmax(s_first)
    for p in range(N_PAIRS):
        lo = p * LANES
        s_cur = s_next
        if p + 2 < N_PAIRS:
            s_next = scores(p + 2)
        ot = _dot(vtw_ref[p], e)
        if p + 1 < N_PAIRS:
            e = softmax(s_cur)
        ot = ot[0:LANES] / ot[LANES:LANES + 1]
        o = jnp.where(row < NA_HEAD_DIM, ot[:, :NA_TQ], ot[:, NA_TQ:])
        o_ref[:, lo:lo + LANES] = o.T.astype(BF16)


def _na_meta_kernel(qm_ref, km_ref, vtm_ref, mb_ref, om_ref):
    first = lax.broadcasted_iota(jnp.int32, (1, LANES), 1) < NA_HEAD_DIM
    for p in range(N_PAIRS):
        lo = p * LANES
        qa = _split_maps(qm_ref[:, lo:lo + LANES])
        kmp = km_ref[:, lo:lo + LANES]
        vtp = vtm_ref[0, lo:lo + LANES, :]
        outs = []
        for half in range(2):
            hd = 2 * p + half
            s = _dot_nt(qa[half * N_META:(half + 1) * N_META], kmp) + mb_ref[hd:hd + 1, :]
            e = jnp.exp2(s - jnp.max(s, axis=-1, keepdims=True))
            pr = e / jnp.sum(e, axis=-1, keepdims=True)
            outs.append(_dot_nt(pr.astype(BF16), vtp))
        om_ref[:, lo:lo + LANES] = jnp.where(first, outs[0], outs[1]).astype(BF16)


def _na_meta_attention(qm, km, vtm, mb):
    nb = vtm.shape[0]
    mrow = pl.BlockSpec((N_META, D_MODEL), lambda b: (b, 0))
    return pl.pallas_call(
        _na_meta_kernel,
        out_shape=jax.ShapeDtypeStruct((nb * N_META, D_MODEL), BF16),
        grid=(nb,),
        in_specs=[mrow, mrow, pl.BlockSpec((1, D_MODEL, N_META), lambda b: (b, 0, 0)), _resident(mb.shape)],
        out_specs=mrow,
        compiler_params=_params("parallel"),
        name="na_meta_attention",
    )(qm, km, vtm, mb)


def _na_bias(rpb, meta_bias):
    i = jnp.arange(NA_RB)[:, None]
    j = jnp.arange(NA_KR)[None, :]
    dr = jnp.stack([j - i, j - i - WIN_ROWS // 2, j - i - WIN_ROWS])
    row_ok = jnp.stack([jnp.broadcast_to(j < WIN_ROWS, (NA_RB, NA_KR)),
                        (j - i >= 0) & (j - i < WIN_ROWS),
                        jnp.broadcast_to(j >= NA_KR - WIN_ROWS, (NA_RB, NA_KR))])
    qc = jnp.arange(GRID_W)[:, None]
    kc = jnp.arange(GRID_W)[None, :]
    c0 = jnp.clip(qc - WIN_COLS // 2, 0, GRID_W - WIN_COLS)
    col_ok = (kc >= c0) & (kc < c0 + WIN_COLS)
    sel_r = jax.nn.one_hot(dr + WIN_ROWS - 1, 2 * WIN_ROWS - 1, dtype=F32)
    sel_c = jax.nn.one_hot(kc - qc + WIN_COLS - 1, 2 * WIN_COLS - 1, dtype=F32)
    r4 = rpb.astype(F32).reshape(N_PAIRS, 2, 2 * WIN_ROWS - 1, 2 * WIN_COLS - 1)
    vals = jnp.einsum("peab,cija,qkb->cpjkeiq", r4, sel_r, sel_c, precision=lax.Precision.HIGHEST)
    ok = (row_ok.transpose(0, 2, 1)[:, None, :, None, None, :, None]
          & col_ok.T[None, None, None, :, None, None, :])
    loc = jnp.where(ok, vals * LOG2E, MASKED).reshape(3, N_PAIRS, NA_TKL, 2 * NA_TQ)
    mb = (meta_bias.astype(F32) * LOG2E).reshape(N_PAIRS, 2, N_META).transpose(0, 2, 1)
    meta = jnp.broadcast_to(mb[None, :, :, :, None], (3, N_PAIRS, N_META, 2, NA_TQ)).reshape(3, N_PAIRS, N_META, 2 * NA_TQ)
    pad = jnp.full((3, N_PAIRS, LANES - N_META, 2 * NA_TQ), MASKED, F32)
    return jnp.concatenate([loc, meta, pad], axis=2)


def _na_attention(q, k, vt, km, vtm, bias, batch, seq, boff):
    n = seq // NA_TQ
    assert seq % NA_TQ == 0 and n >= 3

    def w0(rb):
        return jnp.clip(rb - 1, 0, n - 3)

    def kspec(jj):
        return pl.BlockSpec((NA_TQ, D_MODEL), lambda rb, b: (b * n + w0(rb) + jj, 0))

    def vspec(jj):
        return pl.BlockSpec((1, D_MODEL, NA_TQ), lambda rb, b: (b * n + w0(rb) + jj, 0, 0))

    qo = pl.BlockSpec((NA_TQ, D_MODEL), lambda rb, b: (b * n + rb, 0))
    cls = lambda rb, b: (jnp.where(rb == 0, 0, jnp.where(rb == n - 1, 2, 1)), 0, 0, 0)
    return pl.pallas_call(
        _na_kernel,
        out_shape=jax.ShapeDtypeStruct((batch * seq, D_MODEL), BF16),
        grid=(n, batch),
        in_specs=[qo, kspec(0), kspec(1), kspec(2), vspec(0), vspec(1), vspec(2),
                  pl.BlockSpec((N_META, D_MODEL), lambda rb, b: (boff + b, 0)),
                  pl.BlockSpec((1, D_MODEL, N_META), lambda rb, b: (boff + b, 0, 0)),
                  pl.BlockSpec((1, N_PAIRS, NA_TK, 2 * NA_TQ), cls, pipeline_mode=pl.Buffered(1))],
        out_specs=qo,
        scratch_shapes=[pltpu.VMEM((NA_TK, D_MODEL), BF16), pltpu.VMEM((N_PAIRS, VT_ROWS, NA_TK), BF16)],
        compiler_params=_params("parallel", "parallel"),
        name="na_attention",
    )(q, k, k, k, vt, vt, vt, km, vtm, bias)


def _diff_kernel(q_ref, k_ref, vt_ref, km_ref, vtm_ref, lam_ref, subln_ref, o_ref, acc_ref, s_ref, *,
                 tq, lambda_init):
    nkv, tk = vt_ref.shape[1], vt_ref.shape[3]
    nq = q_ref.shape[1] // tq

    def split_q(qi):
        return _split_maps(q_ref[0, pl.ds(pl.multiple_of(qi * tq, tq), tq), :])

    def scores(qa, kv):
        return _dot_nt(k_ref[0, kv * tk:(kv + 1) * tk, :], qa)

    def start_tile(qa):
        s = _dot_nt(km_ref[0], qa)
        m = jnp.max(s, axis=0, keepdims=True)
        acc_ref[...] = _dot(vtm_ref[0, 0], jnp.exp2(s - m).astype(BF16))
        return m

    lf = lam_ref[...]
    lam = (jnp.exp(jnp.sum(lf[0:1] * lf[1:2], axis=-1, keepdims=True))
           - jnp.exp(jnp.sum(lf[2:3] * lf[3:4], axis=-1, keepdims=True)) + lambda_init)

    def finish_tile(qi):
        on = acc_ref[0:LANES] / acc_ref[LANES:LANES + 1]
        ot = on[:, :tq] - lam * on[:, tq:]
        ms = jnp.mean(ot * ot, axis=0, keepdims=True)
        y = ot * lax.rsqrt(ms + RMS_EPS) * subln_ref[...] * (1.0 - lambda_init)
        o_ref[pl.ds(pl.multiple_of(qi * tq, tq), tq), :] = y.T.astype(BF16)

    qa0 = split_q(0)
    s_ref[0] = scores(qa0, 0)
    m0 = start_tile(qa0)

    def q_tile(qi, m):
        qa = split_q(qi)
        qa_next = split_q(jnp.minimum(qi + 1, nq - 1))
        for kv in range(nkv):
            slot = kv % 2
            s_ref[1 - slot] = scores(qa, kv + 1) if kv + 1 < nkv else scores(qa_next, 0)
            s = s_ref[slot]
            m_new = jnp.maximum(m, jnp.max(s, axis=0, keepdims=True))
            e = jnp.exp2(s - m_new).astype(BF16)
            acc_ref[...] = jnp.exp2(m - m_new) * acc_ref[...] + _dot(vt_ref[0, kv], e)
            m = m_new
        finish_tile(qi)
        return start_tile(qa_next)

    lax.fori_loop(0, nq, q_tile, m0)


def _diff_attention(q, k, vt, km, vtm, lam, subln, batch, seq, boff, tq, nq, q_boff, lambda_init):
    nkv = seq // DIFF_TK
    assert nkv % 2 == 0
    return pl.pallas_call(
        functools.partial(_diff_kernel, tq=tq, lambda_init=lambda_init),
        out_shape=jax.ShapeDtypeStruct((batch * nq * tq, D_MODEL), BF16),
        grid=(batch, DIFF_HEADS),
        in_specs=[pl.BlockSpec((1, nq * tq, LANES), lambda b, h: (h, q_boff + b, 0)),
                  pl.BlockSpec((1, seq, LANES), lambda b, h: (h, b, 0)),
                  pl.BlockSpec((1, nkv, VT_ROWS, DIFF_TK), lambda b, h: (h, b, 0, 0)),
                  pl.BlockSpec((1, N_META, LANES), lambda b, h: (h, boff + b, 0)),
                  pl.BlockSpec((1, 1, VT_ROWS, N_META), lambda b, h: (boff + b, h, 0, 0)),
                  _resident(lam.shape), _resident(subln.shape)],
        out_specs=pl.BlockSpec((nq * tq, LANES), lambda b, h: (b, h)),
        scratch_shapes=[pltpu.VMEM((VT_ROWS, 2 * tq), F32), pltpu.VMEM((2, DIFF_TK, 2 * tq), F32)],
        compiler_params=_params("parallel", "parallel"),
        name="diff_attention",
    )(q, k, vt, km, vtm, lam, subln)


def _rope_tables(pos):
    half = ROPE_DIM // 2
    inv = ROPE_THETA ** (-jnp.arange(0, ROPE_DIM, 2, dtype=F32) / ROPE_DIM)
    ang = pos.astype(F32)[:, None] * inv[None, :]
    cos, sin = jnp.cos(ang), jnp.sin(ang)
    n = pos.shape[0]
    rest = DIFF_HEAD_DIM - ROPE_DIM
    z, zr, one = jnp.zeros((n, half), F32), jnp.zeros((n, rest), F32), jnp.ones((n, rest), F32)
    c = jnp.concatenate([cos, cos, one], axis=1)
    s1 = jnp.concatenate([-sin, z, zr], axis=1)
    s2 = jnp.concatenate([z, sin, zr], axis=1)
    return tuple(jnp.concatenate([t, t], axis=1) for t in (c, s1, s2))


def kernel(x_prompt, x_sample, meta_tokens, norm_g, w_ffn_in, w_ffn_out, w_qkv_a, w_o_a, rpb_a,
           meta_bias_a, w_qkv_b, w_o_b, lambda_b, subln_b):
    xs = (x_prompt, x_sample)
    batches = tuple(x.shape[0] for x in xs)
    seqs = tuple(x.shape[1] for x in xs)
    boffs = (0, batches[0])
    nb = sum(batches)
    depth = norm_g.shape[0]
    for t in seqs:
        assert t % (2 * DIFF_TK) == 0 and t % DIFF_TQ == 0 and t % GRID_W == 0

    hs = [x.reshape(-1, D_MODEL).astype(F32) for x in xs]
    hs.append(jnp.broadcast_to(meta_tokens.astype(F32)[None], (nb, N_META, D_MODEL)).reshape(nb * N_META, D_MODEL))
    tms = (DENSE_TM, DENSE_TM, nb * N_META)

    w_in = w_ffn_in.astype(BF16)
    w_out = w_ffn_out.astype(BF16)
    g = norm_g.astype(F32)[:, :, None, :]

    max_seq = max(seqs)
    real_tabs = _rope_tables(N_META + jnp.arange(max_seq))
    meta_tabs = _rope_tables(jnp.tile(jnp.arange(N_META), nb))

    for i in range(depth):
        j = i // N_MIXERS
        hs = [_ffn(h, g[i, 0], g[i, 1], w_in[i, 0], w_out[i, 0], tm) for h, tm in zip(hs, tms)]
        if i % N_MIXERS == 0:
            w = w_qkv_a[j]
            w_qk = w[:, :2 * D_MODEL].astype(BF16)
            w_vt = w[:, 2 * D_MODEL:].T.astype(BF16)
            w_o = w_o_a[j].astype(BF16)
            qkv = [_qkv(hs[gi], g[i, 2], w_qk, w_vt, DENSE_TM, vt_block=NA_TQ) for gi in range(2)]
            qm, km, vtm = _qkv(hs[2], g[i, 2], w_qk, w_vt, tms[2])
            vtm = vtm.reshape(D_MODEL, nb, N_META).transpose(1, 0, 2)
            bias = _na_bias(rpb_a[j], meta_bias_a[j])
            attn = [_na_attention(*qkv[gi], km, vtm, bias, batches[gi], seqs[gi], boffs[gi]) for gi in range(2)]
            attn.append(_na_meta_attention(qm, km, vtm, meta_bias_a[j].astype(F32) * LOG2E))
        else:
            lambda_init = 0.8 - 0.6 * math.exp(-0.3 * i)
            w = w_qkv_b[j]
            w_qk = w[:, :2 * D_MODEL].astype(BF16)
            w_vt = w[:, 2 * D_MODEL:].T.astype(BF16)
            w_o = w_o_b[j].astype(BF16)
            qkv = [_qkv(hs[gi], g[i, 2], w_qk, w_vt, DIFF_TK, real_tabs, seqs[gi] // DIFF_TK) for gi in range(2)]
            qm, km, vtm = _qkv(hs[2], g[i, 2], w_qk, w_vt, tms[2], meta_tabs, 1)
            vtm = vtm.reshape(N_PAIRS, VT_ROWS, nb, N_META).transpose(2, 0, 1, 3)
            qm_pad = jnp.pad(qm.reshape(N_PAIRS, nb, N_META, LANES),
                             ((0, 0), (0, 0), (0, DIFF_TQ_META - N_META), (0, 0)))
            qm_pad = qm_pad.reshape(N_PAIRS, nb * DIFF_TQ_META, LANES)
            lam = lambda_b[j].astype(F32)
            subln = subln_b[j].astype(F32).reshape(LANES, 1)
            attn, attn_meta = [], []
            for gi in range(2):
                q, k, vt = qkv[gi]
                common = (k, vt, km, vtm, lam, subln, batches[gi], seqs[gi], boffs[gi])
                attn.append(_diff_attention(q, *common, DIFF_TQ, seqs[gi] // DIFF_TQ, 0, lambda_init))
                om = _diff_attention(qm_pad, *common, DIFF_TQ_META, 1, boffs[gi], lambda_init)
                attn_meta.append(om.reshape(batches[gi], DIFF_TQ_META, D_MODEL)[:, :N_META].reshape(-1, D_MODEL))
            attn.append(jnp.concatenate(attn_meta, axis=0))
        hs = [_mix_ffn(h, a, w_o, g[i, 3], g[i, 4], g[i, 5], w_in[i, 1], w_out[i, 1], tm)
              for h, a, tm in zip(hs, attn, tms)]

    return tuple(h.reshape(x.shape).astype(x.dtype) for h, x in zip(hs[:2], xs))
```

```python
import functools
import math

import jax
import jax.numpy as jnp
from jax import lax
from jax.experimental import pallas as pl
from jax.experimental.pallas import tpu as pltpu

F32 = jnp.float32
BF16 = jnp.bfloat16

D_MODEL = 1024
N_META = 16
GRID_W = 64
WIN_ROWS = 8
WIN_COLS = 16
NA_HEADS = 16
NA_HEAD_DIM = D_MODEL // NA_HEADS
DIFF_HEADS = 8
DIFF_HEAD_DIM = D_MODEL // (2 * DIFF_HEADS)
ROPE_THETA = 500000.0
ROPE_DIM = DIFF_HEAD_DIM // 4
RMS_EPS = 1e-6
N_MIXERS = 2

LANES = 128
N_PAIRS = D_MODEL // LANES
VMEM_LIMIT = 56 * 1024 * 1024
MASKED = -1e30
LOG2E = 1.4426950408889634
SUM_ROWS = 16
VT_ROWS = LANES + SUM_ROWS

DENSE_TM = 512

NA_RB = 4
NA_KR = NA_RB + WIN_ROWS
NA_TQ = NA_RB * GRID_W
NA_TKL = NA_KR * GRID_W
NA_TK = NA_TKL + LANES

DIFF_TQ = 512
DIFF_TK = 512
DIFF_TQ_META = 128


def _rms(x, g):
    ms = jnp.mean(x * x, axis=-1, keepdims=True)
    return x * lax.rsqrt(ms + RMS_EPS) * g


def _dot(a, b):
    return jnp.dot(a, b, preferred_element_type=F32)


def _dot_nt(a, b):
    return lax.dot_general(a, b, (((1,), (1,)), ((), ())), preferred_element_type=F32)


def _resident(shape):
    nd = len(shape)
    return pl.BlockSpec(shape, lambda *_: (0,) * nd, pipeline_mode=pl.Buffered(1))


def _params(*sem):
    return pltpu.CompilerParams(dimension_semantics=sem, vmem_limit_bytes=VMEM_LIMIT)


def _split_maps(x):
    first = lax.broadcasted_iota(jnp.int32, (1, LANES), 1) < LANES // 2
    zero = jnp.zeros((), x.dtype)
    return jnp.concatenate([jnp.where(first, x, zero), jnp.where(first, zero, x)], axis=0)


def _half_ffn(x, ga, gb, win_ref, wout_ref):
    xn = _rms(x, ga).astype(BF16)
    gu = _dot(xn, win_ref[...])
    d_ff = wout_ref.shape[0]
    gate = gu[:, :d_ff]
    act = (gate * jax.nn.sigmoid(gate) * gu[:, d_ff:]).astype(BF16)
    y = _dot(act, wout_ref[...])
    return x + 0.5 * _rms(y, gb)


def _mix_ffn_kernel(h_ref, a_ref, wo_ref, gm_ref, ga_ref, gb_ref, win_ref, wout_ref, o_ref):
    x = h_ref[...] + _rms(_dot(a_ref[...], wo_ref[...]), gm_ref[...])
    o_ref[...] = _half_ffn(x, ga_ref[...], gb_ref[...], win_ref, wout_ref)


def _mix_ffn(h, a, w_o, gm, ga, gb, w_in, w_out, tm):
    n = h.shape[0]
    tile = pl.BlockSpec((tm, D_MODEL), lambda i: (i, 0))
    vec = _resident((1, D_MODEL))
    return pl.pallas_call(
        _mix_ffn_kernel,
        out_shape=jax.ShapeDtypeStruct((n, D_MODEL), F32),
        grid=(n // tm,),
        in_specs=[tile, tile, _resident(w_o.shape), vec, vec, vec, _resident(w_in.shape), _resident(w_out.shape)],
        out_specs=tile,
        compiler_params=_params("parallel"),
        name="mix_ffn",
    )(h, a, w_o, gm, ga, gb, w_in, w_out)


def _ffn_qkv_kernel(*refs, rope):
    if rope:
        (h_ref, ga_ref, gb_ref, win_ref, wout_ref, g_ref, wqk_ref, wvt_ref, c_ref, s1_ref, s2_ref,
         o_ref, q_ref, k_ref, vt_ref) = refs
    else:
        h_ref, ga_ref, gb_ref, win_ref, wout_ref, g_ref, wqk_ref, wvt_ref, o_ref, q_ref, k_ref, vt_ref = refs
    h = _half_ffn(h_ref[...], ga_ref[...], gb_ref[...], win_ref, wout_ref)
    o_ref[...] = h
    xn = _rms(h, g_ref[...]).astype(BF16)
    qk = _dot(xn, wqk_ref[...])
    vt = _dot_nt(wvt_ref[...], xn)
    scale = NA_HEAD_DIM ** -0.5 * LOG2E
    if not rope:
        q_ref[...] = (qk[:, :D_MODEL] * scale).astype(BF16)
        k_ref[...] = qk[:, D_MODEL:].astype(BF16)
        if len(vt_ref.shape) == 2:
            vt_ref[...] = vt.astype(BF16)
        else:
            blk = vt_ref.shape[2]
            for c in range(vt_ref.shape[0]):
                vt_ref[c] = vt[:, c * blk:(c + 1) * blk].astype(BF16)
        return
    c, s1, s2 = c_ref[...], s1_ref[...], s2_ref[...]

    def rot(x):
        return x * c + pltpu.roll(x, LANES - ROPE_DIM // 2, 1) * s1 + pltpu.roll(x, ROPE_DIM // 2, 1) * s2

    ones = jnp.ones((SUM_ROWS, vt.shape[1]), BF16)
    for hd in range(N_PAIRS):
        lo = hd * LANES
        q_ref[hd] = (rot(qk[:, lo:lo + LANES]) * scale).astype(BF16)
        k_ref[hd] = rot(qk[:, D_MODEL + lo:D_MODEL + lo + LANES]).astype(BF16)
        vt_ref[hd, 0, 0:LANES] = vt[lo:lo + LANES, :].astype(BF16)
        vt_ref[hd, 0, LANES:VT_ROWS] = ones


def _ffn_qkv(h, ga, gb, w_in, w_out, g, w_qk, w_vt, tm, rope_tables=None, rope_period=None, vt_block=None):
    n = h.shape[0]
    nt = n // tm
    tile = pl.BlockSpec((tm, D_MODEL), lambda i: (i, 0))
    vec = _resident((1, D_MODEL))
    in_specs = [tile, vec, vec, _resident(w_in.shape), _resident(w_out.shape),
                vec, _resident(w_qk.shape), _resident(w_vt.shape)]
    args = [h, ga, gb, w_in, w_out, g, w_qk, w_vt]
    if rope_tables is None:
        if vt_block is None:
            vt_shape, vt_spec = (D_MODEL, n), pl.BlockSpec((D_MODEL, tm), lambda i: (0, i))
        else:
            vt_shape = (n // vt_block, D_MODEL, vt_block)
            vt_spec = pl.BlockSpec((tm // vt_block, D_MODEL, vt_block), lambda i: (i, 0, 0))
        out_shape = (jax.ShapeDtypeStruct((n, D_MODEL), BF16), jax.ShapeDtypeStruct((n, D_MODEL), BF16),
                     jax.ShapeDtypeStruct(vt_shape, BF16))
        out_specs = (tile, tile, vt_spec)
    else:
        tab = pl.BlockSpec((tm, LANES), lambda i: (i % rope_period, 0))
        in_specs += [tab, tab, tab]
        args += list(rope_tables)
        hm = pl.BlockSpec((N_PAIRS, tm, LANES), lambda i: (0, i, 0))
        out_shape = (jax.ShapeDtypeStruct((N_PAIRS, n, LANES), BF16), jax.ShapeDtypeStruct((N_PAIRS, n, LANES), BF16),
                     jax.ShapeDtypeStruct((N_PAIRS, nt, VT_ROWS, tm), BF16))
        out_specs = (hm, hm, pl.BlockSpec((N_PAIRS, 1, VT_ROWS, tm), lambda i: (0, i, 0, 0)))
    return pl.pallas_call(
        functools.partial(_ffn_qkv_kernel, rope=rope_tables is not None),
        out_shape=(jax.ShapeDtypeStruct((n, D_MODEL), F32),) + out_shape,
        grid=(nt,),
        in_specs=in_specs,
        out_specs=(tile,) + out_specs,
        compiler_params=_params("parallel"),
        name="ffn_qkv_rope" if rope_tables is not None else "ffn_qkv",
    )(*args)


def _na_kernel(q_ref, k0_ref, k1_ref, k2_ref, v0_ref, v1_ref, v2_ref, km_ref, vtm_ref, bias_ref, o_ref,
               kw_ref, vtw_ref):
    kw_ref[0:NA_TQ] = k0_ref[...]
    kw_ref[NA_TQ:2 * NA_TQ] = k1_ref[...]
    kw_ref[2 * NA_TQ:NA_TKL] = k2_ref[...]
    kw_ref[NA_TKL:NA_TK] = jnp.zeros((LANES, D_MODEL), BF16)
    kw_ref[NA_TKL:NA_TKL + N_META] = km_ref[...]
    for p in range(N_PAIRS):
        lo = p * LANES
        vtw_ref[p, 0:LANES, 0:NA_TQ] = v0_ref[0, lo:lo + LANES, :]
        vtw_ref[p, 0:LANES, NA_TQ:2 * NA_TQ] = v1_ref[0, lo:lo + LANES, :]
        vtw_ref[p, 0:LANES, 2 * NA_TQ:NA_TKL] = v2_ref[0, lo:lo + LANES, :]
        vtw_ref[p, 0:LANES, NA_TKL:NA_TK] = jnp.zeros((LANES, LANES), BF16)
        vtw_ref[p, 0:LANES, NA_TKL:NA_TKL + N_META] = vtm_ref[0, lo:lo + LANES, :]
        vtw_ref[p, LANES:VT_ROWS, :] = jnp.ones((SUM_ROWS, NA_TK), BF16)

    row = lax.broadcasted_iota(jnp.int32, (LANES, 1), 0)

    def scores(p):
        lo = p * LANES
        qa = _split_maps(q_ref[:, lo:lo + LANES])
        return _dot_nt(kw_ref[:, lo:lo + LANES], qa) + bias_ref[0, p]

    def softmax(s):
        return jnp.exp2(s - jnp.max(s, axis=0, keepdims=True)).astype(BF16)

    s_first = scores(0)
    s_next = scores(1)
    e = softmax(s_first)
    for p in range(N_PAIRS):
        lo = p * LANES
        s_cur = s_next
        if p + 2 < N_PAIRS:
            s_next = scores(p + 2)
        ot = _dot(vtw_ref[p], e)
        if p + 1 < N_PAIRS:
            e = softmax(s_cur)
        ot = ot[0:LANES] / ot[LANES:LANES + 1]
        o = jnp.where(row < NA_HEAD_DIM, ot[:, :NA_TQ], ot[:, NA_TQ:])
        o_ref[:, lo:lo + LANES] = o.T.astype(BF16)


def _na_meta_kernel(qm_ref, km_ref, vtm_ref, mb_ref, om_ref):
    first = lax.broadcasted_iota(jnp.int32, (1, LANES), 1) < NA_HEAD_DIM
    for p in range(N_PAIRS):
        lo = p * LANES
        qa = _split_maps(qm_ref[:, lo:lo + LANES])
        kmp = km_ref[:, lo:lo + LANES]
        vtp = vtm_ref[0, lo:lo + LANES, :]
        outs = []
        for half in range(2):
            hd = 2 * p + half
            s = _dot_nt(qa[half * N_META:(half + 1) * N_META], kmp) + mb_ref[hd:hd + 1, :]
            e = jnp.exp2(s - jnp.max(s, axis=-1, keepdims=True))
            pr = e / jnp.sum(e, axis=-1, keepdims=True)
            outs.append(_dot_nt(pr.astype(BF16), vtp))
        om_ref[:, lo:lo + LANES] = jnp.where(first, outs[0], outs[1]).astype(BF16)


def _na_meta_attention(qm, km, vtm, mb):
    nb = vtm.shape[0]
    mrow = pl.BlockSpec((N_META, D_MODEL), lambda b: (b, 0))
    return pl.pallas_call(
        _na_meta_kernel,
        out_shape=jax.ShapeDtypeStruct((nb * N_META, D_MODEL), BF16),
        grid=(nb,),
        in_specs=[mrow, mrow, pl.BlockSpec((1, D_MODEL, N_META), lambda b: (b, 0, 0)), _resident(mb.shape)],
        out_specs=mrow,
        compiler_params=_params("parallel"),
        name="na_meta_attention",
    )(qm, km, vtm, mb)


def _na_bias(rpb, meta_bias):
    i = jnp.arange(NA_RB)[:, None]
    j = jnp.arange(NA_KR)[None, :]
    dr = jnp.stack([j - i, j - i - WIN_ROWS // 2, j - i - WIN_ROWS])
    row_ok = jnp.stack([jnp.broadcast_to(j < WIN_ROWS, (NA_RB, NA_KR)),
                        (j - i >= 0) & (j - i < WIN_ROWS),
                        jnp.broadcast_to(j >= NA_KR - WIN_ROWS, (NA_RB, NA_KR))])
    qc = jnp.arange(GRID_W)[:, None]
    kc = jnp.arange(GRID_W)[None, :]
    c0 = jnp.clip(qc - WIN_COLS // 2, 0, GRID_W - WIN_COLS)
    col_ok = (kc >= c0) & (kc < c0 + WIN_COLS)
    sel_r = jax.nn.one_hot(dr + WIN_ROWS - 1, 2 * WIN_ROWS - 1, dtype=F32)
    sel_c = jax.nn.one_hot(kc - qc + WIN_COLS - 1, 2 * WIN_COLS - 1, dtype=F32)
    r4 = rpb.astype(F32).reshape(N_PAIRS, 2, 2 * WIN_ROWS - 1, 2 * WIN_COLS - 1)
    vals = jnp.einsum("peab,cija,qkb->cpjkeiq", r4, sel_r, sel_c, precision=lax.Precision.HIGHEST)
    ok = (row_ok.transpose(0, 2, 1)[:, None, :, None, None, :, None]
          & col_ok.T[None, None, None, :, None, None, :])
    loc = jnp.where(ok, vals * LOG2E, MASKED).reshape(3, N_PAIRS, NA_TKL, 2 * NA_TQ)
    mb = (meta_bias.astype(F32) * LOG2E).reshape(N_PAIRS, 2, N_META).transpose(0, 2, 1)
    meta = jnp.broadcast_to(mb[None, :, :, :, None], (3, N_PAIRS, N_META, 2, NA_TQ)).reshape(3, N_PAIRS, N_META, 2 * NA_TQ)
    pad = jnp.full((3, N_PAIRS, LANES - N_META, 2 * NA_TQ), MASKED, F32)
    return jnp.concatenate([loc, meta, pad], axis=2)


def _na_attention(q, k, vt, km, vtm, bias, batch, seq, boff):
    n = seq // NA_TQ
    assert seq % NA_TQ == 0 and n >= 3

    def w0(rb):
        return jnp.clip(rb - 1, 0, n - 3)

    def kspec(jj):
        return pl.BlockSpec((NA_TQ, D_MODEL), lambda rb, b: (b * n + w0(rb) + jj, 0))

    def vspec(jj):
        return pl.BlockSpec((1, D_MODEL, NA_TQ), lambda rb, b: (b * n + w0(rb) + jj, 0, 0))

    qo = pl.BlockSpec((NA_TQ, D_MODEL), lambda rb, b: (b * n + rb, 0))
    cls = lambda rb, b: (jnp.where(rb == 0, 0, jnp.where(rb == n - 1, 2, 1)), 0, 0, 0)
    return pl.pallas_call(
        _na_kernel,
        out_shape=jax.ShapeDtypeStruct((batch * seq, D_MODEL), BF16),
        grid=(n, batch),
        in_specs=[qo, kspec(0), kspec(1), kspec(2), vspec(0), vspec(1), vspec(2),
                  pl.BlockSpec((N_META, D_MODEL), lambda rb, b: (boff + b, 0)),
                  pl.BlockSpec((1, D_MODEL, N_META), lambda rb, b: (boff + b, 0, 0)),
                  pl.BlockSpec((1, N_PAIRS, NA_TK, 2 * NA_TQ), cls, pipeline_mode=pl.Buffered(1))],
        out_specs=qo,
        scratch_shapes=[pltpu.VMEM((NA_TK, D_MODEL), BF16), pltpu.VMEM((N_PAIRS, VT_ROWS, NA_TK), BF16)],
        compiler_params=_params("parallel", "parallel"),
        name="na_attention",
    )(q, k, k, k, vt, vt, vt, km, vtm, bias)


def _diff_kernel(q_ref, k_ref, vt_ref, km_ref, vtm_ref, lam_ref, subln_ref, o_ref, acc_ref, s_ref, *,
                 tq, lambda_init):
    nkv, tk = vt_ref.shape[1], vt_ref.shape[3]
    nq = q_ref.shape[1] // tq

    def split_q(qi):
        return _split_maps(q_ref[0, pl.ds(pl.multiple_of(qi * tq, tq), tq), :])

    def scores(qa, kv, slot):
        s = _dot_nt(k_ref[0, kv * tk:(kv + 1) * tk, :], qa)
        s_ref[slot] = s
        return jnp.max(s, axis=0, keepdims=True)

    def start_tile(qa):
        s = _dot_nt(km_ref[0], qa)
        m = jnp.max(s, axis=0, keepdims=True)
        acc_ref[...] = _dot(vtm_ref[0, 0], jnp.exp2(s - m).astype(BF16))
        return m

    lf = lam_ref[...]
    lam = (jnp.exp(jnp.sum(lf[0:1] * lf[1:2], axis=-1, keepdims=True))
           - jnp.exp(jnp.sum(lf[2:3] * lf[3:4], axis=-1, keepdims=True)) + lambda_init)

    def finish_tile(qi):
        on = acc_ref[0:LANES] / acc_ref[LANES:LANES + 1]
        ot = on[:, :tq] - lam * on[:, tq:]
        ms = jnp.mean(ot * ot, axis=0, keepdims=True)
        y = ot * lax.rsqrt(ms + RMS_EPS) * subln_ref[...] * (1.0 - lambda_init)
        o_ref[pl.ds(pl.multiple_of(qi * tq, tq), tq), :] = y.T.astype(BF16)

    qa0 = split_q(0)
    cmax0 = scores(qa0, 0, 0)
    m0 = start_tile(qa0)

    def q_tile(qi, carry):
        m, cmax = carry
        qa = split_q(qi)
        qa_next = split_q(jnp.minimum(qi + 1, nq - 1))
        for kv in range(nkv):
            slot = kv % 2
            cmax_next = scores(qa, kv + 1, 1 - slot) if kv + 1 < nkv else scores(qa_next, 0, 1 - slot)
            m_new = jnp.maximum(m, cmax)
            e = jnp.exp2(s_ref[slot] - m_new).astype(BF16)
            acc_ref[...] = jnp.exp2(m - m_new) * acc_ref[...] + _dot(vt_ref[0, kv], e)
            m, cmax = m_new, cmax_next
        finish_tile(qi)
        return start_tile(qa_next), cmax

    lax.fori_loop(0, nq, q_tile, (m0, cmax0))


def _diff_attention(q, k, vt, km, vtm, lam, subln, batch, seq, boff, tq, nq, q_boff, lambda_init):
    nkv = seq // DIFF_TK
    assert nkv % 2 == 0
    return pl.pallas_call(
        functools.partial(_diff_kernel, tq=tq, lambda_init=lambda_init),
        out_shape=jax.ShapeDtypeStruct((batch * nq * tq, D_MODEL), BF16),
        grid=(batch, DIFF_HEADS),
        in_specs=[pl.BlockSpec((1, nq * tq, LANES), lambda b, h: (h, q_boff + b, 0)),
                  pl.BlockSpec((1, seq, LANES), lambda b, h: (h, b, 0)),
                  pl.BlockSpec((1, nkv, VT_ROWS, DIFF_TK), lambda b, h: (h, b, 0, 0)),
                  pl.BlockSpec((1, N_META, LANES), lambda b, h: (h, boff + b, 0)),
                  pl.BlockSpec((1, 1, VT_ROWS, N_META), lambda b, h: (boff + b, h, 0, 0)),
                  _resident(lam.shape), _resident(subln.shape)],
        out_specs=pl.BlockSpec((nq * tq, LANES), lambda b, h: (b, h)),
        scratch_shapes=[pltpu.VMEM((VT_ROWS, 2 * tq), F32), pltpu.VMEM((2, DIFF_TK, 2 * tq), F32)],
        compiler_params=_params("parallel", "parallel"),
        name="diff_attention",
    )(q, k, vt, km, vtm, lam, subln)


def _rope_tables(pos):
    half = ROPE_DIM // 2
    inv = ROPE_THETA ** (-jnp.arange(0, ROPE_DIM, 2, dtype=F32) / ROPE_DIM)
    ang = pos.astype(F32)[:, None] * inv[None, :]
    cos, sin = jnp.cos(ang), jnp.sin(ang)
    n = pos.shape[0]
    rest = DIFF_HEAD_DIM - ROPE_DIM
    z, zr, one = jnp.zeros((n, half), F32), jnp.zeros((n, rest), F32), jnp.ones((n, rest), F32)
    c = jnp.concatenate([cos, cos, one], axis=1)
    s1 = jnp.concatenate([-sin, z, zr], axis=1)
    s2 = jnp.concatenate([z, sin, zr], axis=1)
    return tuple(jnp.concatenate([t, t], axis=1) for t in (c, s1, s2))


def kernel(x_prompt, x_sample, meta_tokens, norm_g, w_ffn_in, w_ffn_out, w_qkv_a, w_o_a, rpb_a,
           meta_bias_a, w_qkv_b, w_o_b, lambda_b, subln_b):
    xs = (x_prompt, x_sample)
    batches = tuple(x.shape[0] for x in xs)
    seqs = tuple(x.shape[1] for x in xs)
    boffs = (0, batches[0])
    nb = sum(batches)
    depth = norm_g.shape[0]
    for t in seqs:
        assert t % (2 * DIFF_TK) == 0 and t % DIFF_TQ == 0 and t % GRID_W == 0

    hs = [x.reshape(-1, D_MODEL).astype(F32) for x in xs]
    hs.append(jnp.broadcast_to(meta_tokens.astype(F32)[None], (nb, N_META, D_MODEL)).reshape(nb * N_META, D_MODEL))
    tms = (DENSE_TM, DENSE_TM, nb * N_META)

    w_in = w_ffn_in.astype(BF16)
    w_out = w_ffn_out.astype(BF16)
    g = norm_g.astype(F32)[:, :, None, :]

    max_seq = max(seqs)
    real_tabs = _rope_tables(N_META + jnp.arange(max_seq))
    meta_tabs = _rope_tables(jnp.tile(jnp.arange(N_META), nb))

    for i in range(depth):
        j = i // N_MIXERS
        na_layer = i % N_MIXERS == 0
        w = (w_qkv_a if na_layer else w_qkv_b)[j]
        w_qk = w[:, :2 * D_MODEL].astype(BF16)
        w_vt = w[:, 2 * D_MODEL:].T.astype(BF16)
        w_o = (w_o_a if na_layer else w_o_b)[j].astype(BF16)
        first_half = functools.partial(_ffn_qkv, ga=g[i, 0], gb=g[i, 1], w_in=w_in[i, 0], w_out=w_out[i, 0],
                                       g=g[i, 2], w_qk=w_qk, w_vt=w_vt)
        if na_layer:
            outs = [first_half(hs[gi], tm=DENSE_TM, vt_block=NA_TQ) for gi in range(2)] + [first_half(hs[2], tm=tms[2])]
            hs = [o[0] for o in outs]
            qkv = [o[1:] for o in outs]
            qm, km, vtm = qkv[2]
            vtm = vtm.reshape(D_MODEL, nb, N_META).transpose(1, 0, 2)
            bias = _na_bias(rpb_a[j], meta_bias_a[j])
            attn = [_na_attention(*qkv[gi], km, vtm, bias, batches[gi], seqs[gi], boffs[gi]) for gi in range(2)]
            attn.append(_na_meta_attention(qm, km, vtm, meta_bias_a[j].astype(F32) * LOG2E))
        else:
            lambda_init = 0.8 - 0.6 * math.exp(-0.3 * i)
            outs = [first_half(hs[gi], tm=DIFF_TK, rope_tables=real_tabs, rope_period=seqs[gi] // DIFF_TK)
                    for gi in range(2)]
            outs.append(first_half(hs[2], tm=tms[2], rope_tables=meta_tabs, rope_period=1))
            hs = [o[0] for o in outs]
            qkv = [o[1:] for o in outs]
            qm, km, vtm = qkv[2]
            vtm = vtm.reshape(N_PAIRS, VT_ROWS, nb, N_META).transpose(2, 0, 1, 3)
            qm_pad = jnp.pad(qm.reshape(N_PAIRS, nb, N_META, LANES),
                             ((0, 0), (0, 0), (0, DIFF_TQ_META - N_META), (0, 0)))
            qm_pad = qm_pad.reshape(N_PAIRS, nb * DIFF_TQ_META, LANES)
            lam = lambda_b[j].astype(F32)
            subln = subln_b[j].astype(F32).reshape(LANES, 1)
            attn, attn_meta = [], []
            for gi in range(2):
                q, k, vt = qkv[gi]
                common = (k, vt, km, vtm, lam, subln, batches[gi], seqs[gi], boffs[gi])
                attn.append(_diff_attention(q, *common, DIFF_TQ, seqs[gi] // DIFF_TQ, 0, lambda_init))
                om = _diff_attention(qm_pad, *common, DIFF_TQ_META, 1, boffs[gi], lambda_init)
                attn_meta.append(om.reshape(batches[gi], DIFF_TQ_META, D_MODEL)[:, :N_META].reshape(-1, D_MODEL))
            attn.append(jnp.concatenate(attn_meta, axis=0))
        hs = [_mix_ffn(h, a, w_o, g[i, 3], g[i, 4], g[i, 5], w_in[i, 1], w_out[i, 1], tm)
              for h, a, tm in zip(hs, attn, tms)]

    return tuple(h.reshape(x.shape).astype(x.dtype) for h, x in zip(hs[:2], xs))
```

```python
import functools
import math

import jax
import jax.numpy as jnp
from jax import lax
from jax.experimental import pallas as pl
from jax.experimental.pallas import tpu as pltpu

F32 = jnp.float32
BF16 = jnp.bfloat16

D_MODEL = 1024
N_META = 16
GRID_W = 64
WIN_ROWS = 8
WIN_COLS = 16
NA_HEADS = 16
NA_HEAD_DIM = D_MODEL // NA_HEADS
DIFF_HEADS = 8
DIFF_HEAD_DIM = D_MODEL // (2 * DIFF_HEADS)
ROPE_THETA = 500000.0
ROPE_DIM = DIFF_HEAD_DIM // 4
RMS_EPS = 1e-6
N_MIXERS = 2

LANES = 128
N_PAIRS = D_MODEL // LANES
VMEM_LIMIT = 56 * 1024 * 1024
MASKED = -1e30
LOG2E = 1.4426950408889634
SUM_ROWS = 16
VT_ROWS = LANES + SUM_ROWS

DENSE_TM = 512
DENSE_PARTS = 2

NA_RB = 4
NA_KR = NA_RB + WIN_ROWS
NA_TQ = NA_RB * GRID_W
NA_TKL = NA_KR * GRID_W
NA_TK = NA_TKL + LANES

DIFF_TQ = 512
DIFF_TK = 512
DIFF_TQ_META = 128


def _rms(x, g):
    ms = jnp.mean(x * x, axis=-1, keepdims=True)
    return x * lax.rsqrt(ms + RMS_EPS) * g


def _dot(a, b):
    return jnp.dot(a, b, preferred_element_type=F32)


def _dot_nt(a, b):
    return lax.dot_general(a, b, (((1,), (1,)), ((), ())), preferred_element_type=F32)


def _resident(shape):
    nd = len(shape)
    return pl.BlockSpec(shape, lambda *_: (0,) * nd, pipeline_mode=pl.Buffered(1))


def _resident_slab(shape, idx):
    return pl.BlockSpec((None,) + tuple(shape[1:]), lambda *_: (idx, 0, 0), pipeline_mode=pl.Buffered(1))


def _params(*sem):
    return pltpu.CompilerParams(dimension_semantics=sem, vmem_limit_bytes=VMEM_LIMIT)


def _split_maps(x):
    first = lax.broadcasted_iota(jnp.int32, (1, LANES), 1) < LANES // 2
    zero = jnp.zeros((), x.dtype)
    return jnp.concatenate([jnp.where(first, x, zero), jnp.where(first, zero, x)], axis=0)


def _row_spans(rows):
    parts = DENSE_PARTS if rows % (DENSE_PARTS * LANES) == 0 else 1
    return [(i * (rows // parts), rows // parts) for i in range(parts)]


def _half_ffn(parts, ga, gb, win_ref, wout_ref):
    d_ff = wout_ref.shape[0]
    xn = [_rms(p, ga).astype(BF16) for p in parts]
    gu = [_dot(p, win_ref[...]) for p in xn]
    act = [(p[:, :d_ff] * jax.nn.sigmoid(p[:, :d_ff]) * p[:, d_ff:]).astype(BF16) for p in gu]
    y = [_dot(p, wout_ref[...]) for p in act]
    return [p + 0.5 * _rms(q, gb) for p, q in zip(parts, y)]


def _mix_ffn_kernel(h_ref, a_ref, wo_ref, gm_ref, ga_ref, gb_ref, win_ref, wout_ref, o_ref):
    spans = _row_spans(h_ref.shape[0])
    mixed = [_dot(a_ref[r0:r0 + n], wo_ref[...]) for r0, n in spans]
    x = [h_ref[r0:r0 + n] + _rms(m, gm_ref[...]) for (r0, n), m in zip(spans, mixed)]
    for (r0, n), p in zip(spans, _half_ffn(x, ga_ref[...], gb_ref[...], win_ref, wout_ref)):
        o_ref[r0:r0 + n] = p


def _mix_ffn(h, a, w_o, gm, ga, gb, w_in, w_out, ffn_idx, tm):
    n = h.shape[0]
    tile = pl.BlockSpec((tm, D_MODEL), lambda i: (i, 0))
    vec = _resident((1, D_MODEL))
    return pl.pallas_call(
        _mix_ffn_kernel,
        out_shape=jax.ShapeDtypeStruct((n, D_MODEL), F32),
        grid=(n // tm,),
        in_specs=[tile, tile, _resident(w_o.shape), vec, vec, vec,
                  _resident_slab(w_in.shape, ffn_idx), _resident_slab(w_out.shape, ffn_idx)],
        out_specs=tile,
        compiler_params=_params("parallel"),
        name="mix_ffn",
    )(h, a, w_o, gm, ga, gb, w_in, w_out)


def _ffn_qkv_kernel(*refs, rope):
    if rope:
        (h_ref, ga_ref, gb_ref, win_ref, wout_ref, g_ref, wqk_ref, wvt_ref, c_ref, s1_ref, s2_ref,
         o_ref, q_ref, k_ref, vt_ref) = refs
    else:
        h_ref, ga_ref, gb_ref, win_ref, wout_ref, g_ref, wqk_ref, wvt_ref, o_ref, q_ref, k_ref, vt_ref = refs
    spans = _row_spans(h_ref.shape[0])
    h = _half_ffn([h_ref[r0:r0 + n] for r0, n in spans], ga_ref[...], gb_ref[...], win_ref, wout_ref)
    for (r0, n), p in zip(spans, h):
        o_ref[r0:r0 + n] = p
    xn = [_rms(p, g_ref[...]).astype(BF16) for p in h]
    qk = [_dot(p, wqk_ref[...]) for p in xn]
    vt = [_dot_nt(wvt_ref[...], p) for p in xn]
    scale = NA_HEAD_DIM ** -0.5 * LOG2E
    if not rope:
        for (r0, n), p, t in zip(spans, qk, vt):
            q_ref[r0:r0 + n] = (p[:, :D_MODEL] * scale).astype(BF16)
            k_ref[r0:r0 + n] = p[:, D_MODEL:].astype(BF16)
            if len(vt_ref.shape) == 2:
                vt_ref[:, r0:r0 + n] = t.astype(BF16)
            else:
                blk = vt_ref.shape[2]
                step = min(n, blk)
                for c in range(0, n, step):
                    vt_ref[(r0 + c) // blk, :, (r0 + c) % blk:(r0 + c) % blk + step] = t[:, c:c + step].astype(BF16)
        return

    def rot(x, r0, n):
        return (x * c_ref[r0:r0 + n] + pltpu.roll(x, LANES - ROPE_DIM // 2, 1) * s1_ref[r0:r0 + n]
                + pltpu.roll(x, ROPE_DIM // 2, 1) * s2_ref[r0:r0 + n])

    ones = jnp.ones((SUM_ROWS, h_ref.shape[0]), BF16)
    for hd in range(N_PAIRS):
        lo = hd * LANES
        for (r0, n), p, t in zip(spans, qk, vt):
            q_ref[hd, r0:r0 + n] = (rot(p[:, lo:lo + LANES], r0, n) * scale).astype(BF16)
            k_ref[hd, r0:r0 + n] = rot(p[:, D_MODEL + lo:D_MODEL + lo + LANES], r0, n).astype(BF16)
            vt_ref[hd, 0, 0:LANES, r0:r0 + n] = t[lo:lo + LANES, :].astype(BF16)
        vt_ref[hd, 0, LANES:VT_ROWS] = ones


def _ffn_qkv(h, ga, gb, w_in, w_out, ffn_idx, g, w_qk, w_vt, tm, rope_tables=None, rope_period=None, vt_block=None):
    n = h.shape[0]
    nt = n // tm
    tile = pl.BlockSpec((tm, D_MODEL), lambda i: (i, 0))
    vec = _resident((1, D_MODEL))
    in_specs = [tile, vec, vec, _resident_slab(w_in.shape, ffn_idx), _resident_slab(w_out.shape, ffn_idx),
                vec, _resident(w_qk.shape), _resident(w_vt.shape)]
    args = [h, ga, gb, w_in, w_out, g, w_qk, w_vt]
    if rope_tables is None:
        if vt_block is None:
            vt_shape, vt_spec = (D_MODEL, n), pl.BlockSpec((D_MODEL, tm), lambda i: (0, i))
        else:
            vt_shape = (n // vt_block, D_MODEL, vt_block)
            vt_spec = pl.BlockSpec((tm // vt_block, D_MODEL, vt_block), lambda i: (i, 0, 0))
        out_shape = (jax.ShapeDtypeStruct((n, D_MODEL), BF16), jax.ShapeDtypeStruct((n, D_MODEL), BF16),
                     jax.ShapeDtypeStruct(vt_shape, BF16))
        out_specs = (tile, tile, vt_spec)
    else:
        tab = pl.BlockSpec((tm, LANES), lambda i: (i % rope_period, 0))
        in_specs += [tab, tab, tab]
        args += list(rope_tables)
        hm = pl.BlockSpec((N_PAIRS, tm, LANES), lambda i: (0, i, 0))
        out_shape = (jax.ShapeDtypeStruct((N_PAIRS, n, LANES), BF16), jax.ShapeDtypeStruct((N_PAIRS, n, LANES), BF16),
                     jax.ShapeDtypeStruct((N_PAIRS, nt, VT_ROWS, tm), BF16))
        out_specs = (hm, hm, pl.BlockSpec((N_PAIRS, 1, VT_ROWS, tm), lambda i: (0, i, 0, 0)))
    return pl.pallas_call(
        functools.partial(_ffn_qkv_kernel, rope=rope_tables is not None),
        out_shape=(jax.ShapeDtypeStruct((n, D_MODEL), F32),) + out_shape,
        grid=(nt,),
        in_specs=in_specs,
        out_specs=(tile,) + out_specs,
        compiler_params=_params("parallel"),
        name="ffn_qkv_rope" if rope_tables is not None else "ffn_qkv",
    )(*args)


def _na_kernel(q_ref, k0_ref, k1_ref, k2_ref, v0_ref, v1_ref, v2_ref, km_ref, vtm_ref, bias_ref, o_ref,
               kw_ref, vtw_ref):
    kw_ref[0:NA_TQ] = k0_ref[...]
    kw_ref[NA_TQ:2 * NA_TQ] = k1_ref[...]
    kw_ref[2 * NA_TQ:NA_TKL] = k2_ref[...]
    kw_ref[NA_TKL:NA_TK] = km_ref[0]
    for p in range(N_PAIRS):
        lo = p * LANES
        vtw_ref[p, 0:LANES, 0:NA_TQ] = v0_ref[0, lo:lo + LANES, :]
        vtw_ref[p, 0:LANES, NA_TQ:2 * NA_TQ] = v1_ref[0, lo:lo + LANES, :]
        vtw_ref[p, 0:LANES, 2 * NA_TQ:NA_TKL] = v2_ref[0, lo:lo + LANES, :]
        vtw_ref[p, 0:LANES, NA_TKL:NA_TK] = vtm_ref[0, lo:lo + LANES, :]
        vtw_ref[p, LANES:VT_ROWS, :] = jnp.ones((SUM_ROWS, NA_TK), BF16)

    row = lax.broadcasted_iota(jnp.int32, (LANES, 1), 0)

    def scores(p):
        lo = p * LANES
        qa = _split_maps(q_ref[:, lo:lo + LANES])
        return _dot_nt(kw_ref[:, lo:lo + LANES], qa) + bias_ref[0, p]

    def softmax(s):
        return jnp.exp2(s - jnp.max(s, axis=0, keepdims=True)).astype(BF16)

    s_first = scores(0)
    s_next = scores(1)
    e = softmax(s_first)
    for p in range(N_PAIRS):
        lo = p * LANES
        s_cur = s_next
        if p + 2 < N_PAIRS:
            s_next = scores(p + 2)
        ot = _dot(vtw_ref[p], e)
        if p + 1 < N_PAIRS:
            e = softmax(s_cur)
        ot = ot[0:LANES] / ot[LANES:LANES + 1]
        o = jnp.where(row < NA_HEAD_DIM, ot[:, :NA_TQ], ot[:, NA_TQ:])
        o_ref[:, lo:lo + LANES] = o.T.astype(BF16)


def _na_meta_kernel(qm_ref, km_ref, vtm_ref, mb_ref, om_ref):
    first = lax.broadcasted_iota(jnp.int32, (1, LANES), 1) < NA_HEAD_DIM
    for p in range(N_PAIRS):
        lo = p * LANES
        qa = _split_maps(qm_ref[:, lo:lo + LANES])
        kmp = km_ref[:, lo:lo + LANES]
        vtp = vtm_ref[0, lo:lo + LANES, :]
        outs = []
        for half in range(2):
            hd = 2 * p + half
            s = _dot_nt(qa[half * N_META:(half + 1) * N_META], kmp) + mb_ref[hd:hd + 1, :]
            e = jnp.exp2(s - jnp.max(s, axis=-1, keepdims=True))
            pr = e / jnp.sum(e, axis=-1, keepdims=True)
            outs.append(_dot_nt(pr.astype(BF16), vtp))
        om_ref[:, lo:lo + LANES] = jnp.where(first, outs[0], outs[1]).astype(BF16)


def _na_meta_attention(qm, km, vtm, mb):
    nb = vtm.shape[0]
    mrow = pl.BlockSpec((N_META, D_MODEL), lambda b: (b, 0))
    return pl.pallas_call(
        _na_meta_kernel,
        out_shape=jax.ShapeDtypeStruct((nb * N_META, D_MODEL), BF16),
        grid=(nb,),
        in_specs=[mrow, mrow, pl.BlockSpec((1, D_MODEL, N_META), lambda b: (b, 0, 0)), _resident(mb.shape)],
        out_specs=mrow,
        compiler_params=_params("parallel"),
        name="na_meta_attention",
    )(qm, km, vtm, mb)


def _na_bias(rpb, meta_bias):
    i = jnp.arange(NA_RB)[:, None]
    j = jnp.arange(NA_KR)[None, :]
    dr = jnp.stack([j - i, j - i - WIN_ROWS // 2, j - i - WIN_ROWS])
    row_ok = jnp.stack([jnp.broadcast_to(j < WIN_ROWS, (NA_RB, NA_KR)),
                        (j - i >= 0) & (j - i < WIN_ROWS),
                        jnp.broadcast_to(j >= NA_KR - WIN_ROWS, (NA_RB, NA_KR))])
    qc = jnp.arange(GRID_W)[:, None]
    kc = jnp.arange(GRID_W)[None, :]
    c0 = jnp.clip(qc - WIN_COLS // 2, 0, GRID_W - WIN_COLS)
    col_ok = (kc >= c0) & (kc < c0 + WIN_COLS)
    sel_r = jax.nn.one_hot(dr + WIN_ROWS - 1, 2 * WIN_ROWS - 1, dtype=F32)
    sel_c = jax.nn.one_hot(kc - qc + WIN_COLS - 1, 2 * WIN_COLS - 1, dtype=F32)
    r4 = rpb.astype(F32).reshape(N_PAIRS, 2, 2 * WIN_ROWS - 1, 2 * WIN_COLS - 1)
    vals = jnp.einsum("peab,cija,qkb->cpjkeiq", r4, sel_r, sel_c, precision=lax.Precision.HIGHEST)
    ok = (row_ok.transpose(0, 2, 1)[:, None, :, None, None, :, None]
          & col_ok.T[None, None, None, :, None, None, :])
    loc = jnp.where(ok, vals * LOG2E, MASKED).reshape(3, N_PAIRS, NA_TKL, 2 * NA_TQ)
    mb = (meta_bias.astype(F32) * LOG2E).reshape(N_PAIRS, 2, N_META).transpose(0, 2, 1)
    meta = jnp.broadcast_to(mb[None, :, :, :, None], (3, N_PAIRS, N_META, 2, NA_TQ)).reshape(3, N_PAIRS, N_META, 2 * NA_TQ)
    pad = jnp.full((3, N_PAIRS, LANES - N_META, 2 * NA_TQ), MASKED, F32)
    return jnp.concatenate([loc, meta, pad], axis=2)


def _na_attention(q, k, vt, km, vtm, bias, batch, seq, boff):
    n = seq // NA_TQ
    assert seq % NA_TQ == 0 and n >= 3

    def w0(rb):
        return jnp.clip(rb - 1, 0, n - 3)

    def kspec(jj):
        return pl.BlockSpec((NA_TQ, D_MODEL), lambda rb, b: (b * n + w0(rb) + jj, 0))

    def vspec(jj):
        return pl.BlockSpec((1, D_MODEL, NA_TQ), lambda rb, b: (b * n + w0(rb) + jj, 0, 0))

    qo = pl.BlockSpec((NA_TQ, D_MODEL), lambda rb, b: (b * n + rb, 0))
    cls = lambda rb, b: (jnp.where(rb == 0, 0, jnp.where(rb == n - 1, 2, 1)), 0, 0, 0)
    return pl.pallas_call(
        _na_kernel,
        out_shape=jax.ShapeDtypeStruct((batch * seq, D_MODEL), BF16),
        grid=(n, batch),
        in_specs=[qo, kspec(0), kspec(1), kspec(2), vspec(0), vspec(1), vspec(2),
                  pl.BlockSpec((1, LANES, D_MODEL), lambda rb, b: (boff + b, 0, 0)),
                  pl.BlockSpec((1, D_MODEL, LANES), lambda rb, b: (boff + b, 0, 0)),
                  pl.BlockSpec((1, N_PAIRS, NA_TK, 2 * NA_TQ), cls, pipeline_mode=pl.Buffered(1))],
        out_specs=qo,
        scratch_shapes=[pltpu.VMEM((NA_TK, D_MODEL), BF16), pltpu.VMEM((N_PAIRS, VT_ROWS, NA_TK), BF16)],
        compiler_params=_params("parallel", "parallel"),
        name="na_attention",
    )(q, k, k, k, vt, vt, vt, km, vtm, bias)


def _diff_kernel(q_ref, k_ref, vt_ref, km_ref, vtm_ref, lam_ref, subln_ref, o_ref, acc_ref, s_ref, *,
                 tq, lambda_init):
    nkv, tk = vt_ref.shape[1], vt_ref.shape[3]
    nq = q_ref.shape[1] // tq

    def split_q(qi):
        return _split_maps(q_ref[0, pl.ds(pl.multiple_of(qi * tq, tq), tq), :])

    def scores(qa, kv, slot):
        s = _dot_nt(k_ref[0, kv * tk:(kv + 1) * tk, :], qa)
        s_ref[slot] = s
        return jnp.max(s, axis=0, keepdims=True)

    def start_tile(qa):
        s = _dot_nt(km_ref[0], qa)
        m = jnp.max(s, axis=0, keepdims=True)
        acc_ref[...] = _dot(vtm_ref[0, 0], jnp.exp2(s - m).astype(BF16))
        return m

    lf = lam_ref[...]
    lam = (jnp.exp(jnp.sum(lf[0:1] * lf[1:2], axis=-1, keepdims=True))
           - jnp.exp(jnp.sum(lf[2:3] * lf[3:4], axis=-1, keepdims=True)) + lambda_init)

    def finish_tile(qi):
        on = acc_ref[0:LANES] / acc_ref[LANES:LANES + 1]
        ot = on[:, :tq] - lam * on[:, tq:]
        ms = jnp.mean(ot * ot, axis=0, keepdims=True)
        y = ot * lax.rsqrt(ms + RMS_EPS) * subln_ref[...] * (1.0 - lambda_init)
        o_ref[pl.ds(pl.multiple_of(qi * tq, tq), tq), :] = y.T.astype(BF16)

    qa0 = split_q(0)
    cmax0 = scores(qa0, 0, 0)
    m0 = start_tile(qa0)

    def q_tile(qi, carry):
        m, cmax = carry
        qa = split_q(qi)
        qa_next = split_q(jnp.minimum(qi + 1, nq - 1))
        for kv in range(nkv):
            slot = kv % 2
            cmax_next = scores(qa, kv + 1, 1 - slot) if kv + 1 < nkv else scores(qa_next, 0, 1 - slot)
            m_new = jnp.maximum(m, cmax)
            alpha = jnp.exp2(m - m_new)
            for c0 in (0, tq):
                cols = slice(c0, c0 + tq)
                e = jnp.exp2(s_ref[slot, :, cols] - m_new[:, cols]).astype(BF16)
                acc_ref[:, cols] = alpha[:, cols] * acc_ref[:, cols] + _dot(vt_ref[0, kv], e)
            m, cmax = m_new, cmax_next
        finish_tile(qi)
        return start_tile(qa_next), cmax

    lax.fori_loop(0, nq, q_tile, (m0, cmax0))


def _diff_attention(q, k, vt, km, vtm, lam, subln, batch, seq, boff, tq, nq, q_boff, lambda_init):
    nkv = seq // DIFF_TK
    assert nkv % 2 == 0
    return pl.pallas_call(
        functools.partial(_diff_kernel, tq=tq, lambda_init=lambda_init),
        out_shape=jax.ShapeDtypeStruct((batch * nq * tq, D_MODEL), BF16),
        grid=(batch, DIFF_HEADS),
        in_specs=[pl.BlockSpec((1, nq * tq, LANES), lambda b, h: (h, q_boff + b, 0)),
                  pl.BlockSpec((1, seq, LANES), lambda b, h: (h, b, 0)),
                  pl.BlockSpec((1, nkv, VT_ROWS, DIFF_TK), lambda b, h: (h, b, 0, 0)),
                  pl.BlockSpec((1, N_META, LANES), lambda b, h: (h, boff + b, 0)),
                  pl.BlockSpec((1, 1, VT_ROWS, N_META), lambda b, h: (boff + b, h, 0, 0)),
                  _resident(lam.shape), _resident(subln.shape)],
        out_specs=pl.BlockSpec((nq * tq, LANES), lambda b, h: (b, h)),
        scratch_shapes=[pltpu.VMEM((VT_ROWS, 2 * tq), F32), pltpu.VMEM((2, DIFF_TK, 2 * tq), F32)],
        compiler_params=_params("parallel", "parallel"),
        name="diff_attention",
    )(q, k, vt, km, vtm, lam, subln)


def _rope_tables(pos):
    half = ROPE_DIM // 2
    inv = ROPE_THETA ** (-jnp.arange(0, ROPE_DIM, 2, dtype=F32) / ROPE_DIM)
    within = jnp.arange(LANES) % DIFF_HEAD_DIM
    inv_lane = jnp.where(within < ROPE_DIM, inv[within % half], 0.0)
    ang = pos.astype(F32)[:, None] * inv_lane[None, :]
    cos, sin = jnp.cos(ang), jnp.sin(ang)
    s1 = jnp.where(within < half, -sin, 0.0)
    s2 = jnp.where((within >= half) & (within < ROPE_DIM), sin, 0.0)
    return cos, s1, s2


def kernel(x_prompt, x_sample, meta_tokens, norm_g, w_ffn_in, w_ffn_out, w_qkv_a, w_o_a, rpb_a,
           meta_bias_a, w_qkv_b, w_o_b, lambda_b, subln_b):
    xs = (x_prompt, x_sample)
    batches = tuple(x.shape[0] for x in xs)
    seqs = tuple(x.shape[1] for x in xs)
    boffs = (0, batches[0])
    nb = sum(batches)
    depth = norm_g.shape[0]
    for t in seqs:
        assert t % (2 * DIFF_TK) == 0 and t % DIFF_TQ == 0 and t % GRID_W == 0

    hs = [x.reshape(-1, D_MODEL).astype(F32) for x in xs]
    hs.append(jnp.broadcast_to(meta_tokens.astype(F32)[None], (nb, N_META, D_MODEL)).reshape(nb * N_META, D_MODEL))
    tms = (DENSE_TM, DENSE_TM, nb * N_META)

    w_in = w_ffn_in.astype(BF16).reshape((2 * depth,) + w_ffn_in.shape[2:])
    w_out = w_ffn_out.astype(BF16).reshape((2 * depth,) + w_ffn_out.shape[2:])
    g = norm_g.astype(F32)[:, :, None, :]

    max_seq = max(seqs)
    real_tabs = _rope_tables(N_META + jnp.arange(max_seq))
    meta_tabs = _rope_tables(jnp.tile(jnp.arange(N_META), nb))

    for i in range(depth):
        j = i // N_MIXERS
        na_layer = i % N_MIXERS == 0
        w = (w_qkv_a if na_layer else w_qkv_b)[j]
        w_qk = w[:, :2 * D_MODEL].astype(BF16)
        w_vt = w[:, 2 * D_MODEL:].T.astype(BF16)
        w_o = (w_o_a if na_layer else w_o_b)[j].astype(BF16)
        first_half = functools.partial(_ffn_qkv, ga=g[i, 0], gb=g[i, 1], w_in=w_in, w_out=w_out, ffn_idx=2 * i,
                                       g=g[i, 2], w_qk=w_qk, w_vt=w_vt)
        if na_layer:
            outs = [first_half(hs[gi], tm=DENSE_TM, vt_block=NA_TQ) for gi in range(2)] + [first_half(hs[2], tm=tms[2])]
            hs = [o[0] for o in outs]
            qkv = [o[1:] for o in outs]
            qm, km, vtm = qkv[2]
            vtm = vtm.reshape(D_MODEL, nb, N_META).transpose(1, 0, 2)
            bias = _na_bias(rpb_a[j], meta_bias_a[j])
            km_pad = jnp.pad(km.reshape(nb, N_META, D_MODEL), ((0, 0), (0, LANES - N_META), (0, 0)))
            vtm_pad = jnp.pad(vtm, ((0, 0), (0, 0), (0, LANES - N_META)))
            attn = [_na_attention(*qkv[gi], km_pad, vtm_pad, bias, batches[gi], seqs[gi], boffs[gi])
                    for gi in range(2)]
            attn.append(_na_meta_attention(qm, km, vtm, meta_bias_a[j].astype(F32) * LOG2E))
        else:
            lambda_init = 0.8 - 0.6 * math.exp(-0.3 * i)
            outs = [first_half(hs[gi], tm=DIFF_TK, rope_tables=real_tabs, rope_period=seqs[gi] // DIFF_TK)
                    for gi in range(2)]
            outs.append(first_half(hs[2], tm=tms[2], rope_tables=meta_tabs, rope_period=1))
            hs = [o[0] for o in outs]
            qkv = [o[1:] for o in outs]
            qm, km, vtm = qkv[2]
            vtm = vtm.reshape(N_PAIRS, VT_ROWS, nb, N_META).transpose(2, 0, 1, 3)
            qm_pad = jnp.pad(qm.reshape(N_PAIRS, nb, N_META, LANES),
                             ((0, 0), (0, 0), (0, DIFF_TQ_META - N_META), (0, 0)))
            qm_pad = qm_pad.reshape(N_PAIRS, nb * DIFF_TQ_META, LANES)
            lam = lambda_b[j].astype(F32)
            subln = subln_b[j].astype(F32).reshape(LANES, 1)
            attn, attn_meta = [], []
            for gi in range(2):
                q, k, vt = qkv[gi]
                common = (k, vt, km, vtm, lam, subln, batches[gi], seqs[gi], boffs[gi])
                attn.append(_diff_attention(q, *common, DIFF_TQ, seqs[gi] // DIFF_TQ, 0, lambda_init))
                om = _diff_attention(qm_pad, *common, DIFF_TQ_META, 1, boffs[gi], lambda_init)
                attn_meta.append(om.reshape(batches[gi], DIFF_TQ_META, D_MODEL)[:, :N_META].reshape(-1, D_MODEL))
            attn.append(jnp.concatenate(attn_meta, axis=0))
        hs = [_mix_ffn(h, a, w_o, g[i, 3], g[i, 4], g[i, 5], w_in, w_out, 2 * i + 1, tm)
              for h, a, tm in zip(hs, attn, tms)]

    return tuple(h.reshape(x.shape).astype(x.dtype) for h, x in zip(hs[:2], xs))
```
